```python
import functools
import jax
import jax.numpy as jnp
from jax import lax
import numpy as np

D_MODEL = 4096
BATCH = 4
SEQ = 2048
DEPTH = 1
DEC_BATCH = 32
DEC_SEQ = 1
PAST_LEN = 8192
PAGE_SIZE = 128

D_ATT = D_MODEL // 2
HD_ATT = 128
H_ATT = D_ATT // HD_ATT
D_RWKV = D_MODEL - D_ATT
HD_RWKV = 64
H_RWKV = D_RWKV // HD_RWKV
MOBA_BLOCK = 256
MOBA_TOPK = 3
Q_CHUNK = 32
ROPE_THETA = 10000.0
D_FF = ((8 * D_MODEL // 3 + 255) // 256) * 256
LORA_W = max(32, int(round(1.8 * D_RWKV ** 0.5 / 32)) * 32)
LORA_A = max(32, int(round(1.8 * D_RWKV ** 0.5 / 32)) * 32)
LORA_G = max(32, int(round(0.6 * D_RWKV ** 0.8 / 32)) * 32)
N_MOD = 9
RMS_EPS = 1e-6
GN_EPS = 64e-5

kernel_name = "hymba_moba_rwkv7_macaron_adaln_step"


def rmsnorm(x, g):
    xf = x.astype(jnp.float32)
    y = xf * lax.rsqrt(jnp.mean(xf * xf, axis=-1, keepdims=True) + RMS_EPS)
    return (y * g.astype(jnp.float32)).astype(x.dtype)


def modulate(x, g, shift, scale):
    return rmsnorm(x, g) * (1 + scale[:, None]) + shift[:, None]


def swiglu(h, wg, wu, wd):
    return (jax.nn.silu(h @ wg) * (h @ wu)) @ wd


def rope(x, pos):
    half = HD_ATT // 2
    inv = ROPE_THETA ** (-jnp.arange(half, dtype=jnp.float32) / half)
    ang = pos.astype(jnp.float32)[:, None] * inv[None, :]
    cos = jnp.cos(ang)[None, :, None, :]
    sin = jnp.sin(ang)[None, :, None, :]
    x1 = x[..., :half].astype(jnp.float32)
    x2 = x[..., half:].astype(jnp.float32)
    return jnp.concatenate([x1 * cos - x2 * sin, x2 * cos + x1 * sin], axis=-1).astype(x.dtype)


def blockify(parts):
    k = jnp.concatenate(parts, axis=1)
    B, T, H, hd = k.shape
    nb = -(-T // MOBA_BLOCK)
    pad = nb * MOBA_BLOCK - T
    if pad:
        k = jnp.concatenate([k, jnp.zeros((B, pad, H, hd), k.dtype)], axis=1)
    return k.reshape(B, nb, MOBA_BLOCK, H, hd)


def moba_chunk(q, qpos, kb, vb, kmean):
    B, Qc, H, hd = q.shape
    nb = kb.shape[1]
    k_sel = min(MOBA_TOPK, nb)
    own = qpos // MOBA_BLOCK
    gate = jnp.einsum('bqhd,bnhd->bqhn', q.astype(jnp.float32), kmean)
    past = jnp.arange(nb, dtype=jnp.int32)[None, :] < own[:, None]
    gate = jnp.where(past[None, :, None, :], gate, -jnp.inf)
    _, top_idx = lax.top_k(gate, k_sel)
    top_valid = jnp.arange(k_sel, dtype=jnp.int32)[None, :] < own[:, None]
    own_idx = jnp.broadcast_to(own[None, :, None, None], (B, Qc, H, 1)).astype(top_idx.dtype)
    idx = jnp.concatenate([top_idx, own_idx], axis=-1)
    valid = jnp.concatenate([top_valid, jnp.ones((Qc, 1), bool)], axis=-1)[None, :, None, :]
    b_i = jnp.arange(B)[:, None, None, None]
    h_i = jnp.arange(H)[None, None, :, None]
    kg = kb[b_i, idx, :, h_i]
    vg = vb[b_i, idx, :, h_i]
    s = jnp.einsum('bqhd,bqhnkd->bqhnk', q, kg, preferred_element_type=jnp.float32) * (HD_ATT ** -0.5)
    kpos = idx[..., None] * MOBA_BLOCK + jnp.arange(MOBA_BLOCK, dtype=jnp.int32)
    mask = valid[..., None] & (kpos <= qpos[None, :, None, None, None])
    s = jnp.where(mask, s, -jnp.inf)
    p = jax.nn.softmax(s.reshape(B, Qc, H, -1), axis=-1).reshape(s.shape)
    return jnp.einsum('bqhnk,bqhnkd->bqhd', p.astype(vg.dtype), vg)


def moba_prompt(q, k, v, qpos):
    B, T, H, hd = q.shape
    kb = blockify([k])
    vb = blockify([v])
    kmean = kb.astype(jnp.float32).mean(axis=2)
    n_chunks = T // Q_CHUNK
    qc = q.reshape(B, n_chunks, Q_CHUNK, H, hd).swapaxes(0, 1)
    pc = qpos.reshape(n_chunks, Q_CHUNK)
    out = lax.map(lambda a: moba_chunk(a[0], a[1], kb, vb, kmean), (qc, pc))
    return out.swapaxes(0, 1).reshape(B, T, H, hd)


def moba_sample(q, k, v, qpos, k_past, v_past):
    kb = blockify([k_past, k])
    vb = blockify([v_past, v])
    kmean = kb.astype(jnp.float32).mean(axis=2)
    return moba_chunk(q, qpos, kb, vb, kmean)


def rwkv7_mix(h, h_prev, p_rkv, p_prev, S0, lp):
    B, T, _ = h.shape
    f32 = jnp.float32
    ps = p_rkv + (p_prev - p_rkv) * lp['mu_rkv'].reshape(3 * D_RWKV)
    r, k, v = jnp.split(ps, 3, axis=-1)
    dx = jnp.concatenate([h_prev[:, None], h[:, :-1]], axis=1) - h
    xw = h + dx * lp['mu_wag'][0]
    xa = h + dx * lp['mu_wag'][1]
    xg = h + dx * lp['mu_wag'][2]
    w = -jax.nn.softplus(-(lp['w0'] + jnp.tanh(xw @ lp['w_lora1']) @ lp['w_lora2']).astype(f32)) - 0.5
    decay = jnp.exp(-jnp.exp(w))
    a = jax.nn.sigmoid((lp['a0'] + (xa @ lp['a_lora1']) @ lp['a_lora2']).astype(f32))
    g = jax.nn.sigmoid(xg @ lp['g_lora1']) @ lp['g_lora2']

    def heads(t):
        return t.astype(f32).reshape(B, T, H_RWKV, HD_RWKV)

    kk = heads(k * lp['k_k'])
    kk = kk * lax.rsqrt(jnp.maximum(jnp.sum(kk * kk, axis=-1, keepdims=True), 1e-24))
    k = k.astype(f32) * (1 + (a - 1) * lp['k_a'])
    rh, wh, kh, vh, ah = heads(r), heads(decay), heads(k), heads(v), heads(a)

    def tm(t):
        return jnp.swapaxes(t, 0, 1)

    def step(S, inp):
        r_t, w_t, k_t, v_t, kk_t, a_t = inp
        sa = jnp.einsum('bhij,bhj->bhi', S, -kk_t)
        S = S * w_t[:, :, None, :] + sa[..., None] * (kk_t * a_t)[:, :, None, :] + v_t[..., None] * k_t[:, :, None, :]
        return S, jnp.einsum('bhij,bhj->bhi', S, r_t)

    S_fin, y = lax.scan(step, S0.astype(f32), (tm(rh), tm(wh), tm(kh), tm(vh), tm(kk), tm(ah)))
    y = tm(y)
    mean = jnp.mean(y, axis=-1, keepdims=True)
    var = jnp.mean(jnp.square(y - mean), axis=-1, keepdims=True)
    yn = ((y - mean) * lax.rsqrt(var + GN_EPS)).reshape(B, T, D_RWKV) * lp['ln_x_w'] + lp['ln_x_b']
    bonus = (jnp.sum(rh * kh * lp['r_k'], axis=-1, keepdims=True) * vh).reshape(B, T, D_RWKV)
    return ((yn + bonus) * g).astype(h.dtype), S_fin


def block(x, c, pos, h_prev, S0, lp, attend):
    B, T, _ = x.shape
    mod = jax.nn.silu(c) @ lp['w_ada'] + lp['b_ada']
    sh1, sc1, ga1, sh2, sc2, ga2, sh3, sc3, ga3 = jnp.split(mod, N_MOD, axis=-1)
    h1 = modulate(x, lp['g_norm'][0], sh1, sc1)
    x = x + 0.5 * ga1[:, None] * swiglu(h1, lp['ffn_w_gate'][0], lp['ffn_w_up'][0], lp['ffn_w_down'][0])
    h = modulate(x, lp['g_norm'][1], sh2, sc2)
    proj = h @ lp['w_in']
    q = rope(proj[..., :D_ATT].reshape(B, T, H_ATT, HD_ATT), pos)
    k = rope(proj[..., D_ATT:2 * D_ATT].reshape(B, T, H_ATT, HD_ATT), pos)
    v = proj[..., 2 * D_ATT:3 * D_ATT].reshape(B, T, H_ATT, HD_ATT)
    p_rkv = proj[..., 3 * D_ATT:]
    att = attend(q, k, v, pos).reshape(B, T, D_ATT)
    p_first = h_prev @ lp['w_in'][:, 3 * D_ATT:]
    p_prev = jnp.concatenate([p_first[:, None], p_rkv[:, :-1]], axis=1)
    rw, S_new = rwkv7_mix(h, h_prev, p_rkv, p_prev, S0, lp)
    x = x + ga2[:, None] * (jnp.concatenate([att, rw], axis=-1) @ lp['w_out'])
    h3 = modulate(x, lp['g_norm'][2], sh3, sc3)
    x = x + 0.5 * ga3[:, None] * swiglu(h3, lp['ffn_w_gate'][1], lp['ffn_w_up'][1], lp['ffn_w_down'][1])
    return x, k, v, h[:, -1], S_new


def setup_inputs(seed: int = 0) -> dict:
    key = jax.random.key(seed)
    ks = iter(jax.random.split(key, 48))
    f32 = jnp.float32

    def nrm(shape, scale):
        return scale * jax.random.normal(next(ks), shape, f32)

    n_pages = PAST_LEN // PAGE_SIZE
    n_pool = (5 * DEC_BATCH * n_pages) // 4
    d_in = 3 * D_ATT + 3 * D_RWKV
    x_prompt = nrm((BATCH, SEQ, D_MODEL), 1.0)
    x_sample = nrm((DEC_BATCH, DEC_SEQ, D_MODEL), 1.0)
    cache_k = nrm((DEPTH, n_pool, PAGE_SIZE, H_ATT, HD_ATT), 1.0)
    cache_v = nrm((DEPTH, n_pool, PAGE_SIZE, H_ATT, HD_ATT), 1.0)
    state_shift = nrm((DEPTH, DEC_BATCH, D_MODEL), 1.0)
    state_wkv = nrm((DEPTH, DEC_BATCH, H_RWKV, HD_RWKV, HD_RWKV), 0.5)
    perm = jax.random.permutation(next(ks), n_pool)
    page_table = perm[:DEC_BATCH * n_pages].reshape(DEC_BATCH, n_pages).astype(jnp.int32)
    c_prompt = nrm((BATCH, D_MODEL), 1.0)
    c_sample = nrm((DEC_BATCH, D_MODEL), 1.0)
    g_norm = 1.0 + nrm((DEPTH, 3, D_MODEL), 0.05)
    w_ada = nrm((DEPTH, D_MODEL, N_MOD * D_MODEL), 0.3 * D_MODEL ** -0.5)
    b_ada = nrm((DEPTH, N_MOD * D_MODEL), 0.02)
    ffn_w_gate = nrm((DEPTH, 2, D_MODEL, D_FF), D_MODEL ** -0.5)
    ffn_w_up = nrm((DEPTH, 2, D_MODEL, D_FF), D_MODEL ** -0.5)
    ffn_w_down = nrm((DEPTH, 2, D_FF, D_MODEL), D_FF ** -0.5)
    w_in = nrm((DEPTH, D_MODEL, d_in), D_MODEL ** -0.5)
    w_out = nrm((DEPTH, D_ATT + D_RWKV, D_MODEL), (D_ATT + D_RWKV) ** -0.5)
    mu_rkv = jax.random.uniform(next(ks), (DEPTH, 3, D_RWKV), f32)
    mu_wag = jax.random.uniform(next(ks), (DEPTH, 3, D_MODEL), f32)
    w0 = jax.random.uniform(next(ks), (DEPTH, D_RWKV), f32, -3.0, 1.0)
    w_lora1 = nrm((DEPTH, D_MODEL, LORA_W), D_MODEL ** -0.5)
    w_lora2 = nrm((DEPTH, LORA_W, D_RWKV), 0.3 * LORA_W ** -0.5)
    a0 = nrm((DEPTH, D_RWKV), 0.3)
    a_lora1 = nrm((DEPTH, D_MODEL, LORA_A), D_MODEL ** -0.5)
    a_lora2 = nrm((DEPTH, LORA_A, D_RWKV), 0.3 * LORA_A ** -0.5)
    g_lora1 = nrm((DEPTH, D_MODEL, LORA_G), D_MODEL ** -0.5)
    g_lora2 = nrm((DEPTH, LORA_G, D_RWKV), LORA_G ** -0.5)
    k_k = 0.85 + nrm((DEPTH, D_RWKV), 0.05)
    k_a = 1.0 + nrm((DEPTH, D_RWKV), 0.05)
    r_k = nrm((DEPTH, H_RWKV, HD_RWKV), 0.1)
    ln_x_w = 1.0 + nrm((DEPTH, D_RWKV), 0.05)
    ln_x_b = nrm((DEPTH, D_RWKV), 0.02)
    g_final = 1.0 + nrm((D_MODEL,), 0.05)
    return {"x_prompt": x_prompt, "x_sample": x_sample, "cache_k": cache_k, "cache_v": cache_v,
            "state_shift": state_shift, "state_wkv": state_wkv, "page_table": page_table,
            "c_prompt": c_prompt, "c_sample": c_sample, "g_norm": g_norm, "w_ada": w_ada,
            "b_ada": b_ada, "ffn_w_gate": ffn_w_gate, "ffn_w_up": ffn_w_up, "ffn_w_down": ffn_w_down,
            "w_in": w_in, "w_out": w_out, "mu_rkv": mu_rkv, "mu_wag": mu_wag, "w0": w0,
            "w_lora1": w_lora1, "w_lora2": w_lora2, "a0": a0, "a_lora1": a_lora1, "a_lora2": a_lora2,
            "g_lora1": g_lora1, "g_lora2": g_lora2, "k_k": k_k, "k_a": k_a, "r_k": r_k,
            "ln_x_w": ln_x_w, "ln_x_b": ln_x_b, "g_final": g_final}


def reference(x_prompt, x_sample, cache_k, cache_v, state_shift, state_wkv, page_table,
              c_prompt, c_sample, g_norm, w_ada, b_ada, ffn_w_gate, ffn_w_up, ffn_w_down,
              w_in, w_out, mu_rkv, mu_wag, w0, w_lora1, w_lora2, a0, a_lora1, a_lora2,
              g_lora1, g_lora2, k_k, k_a, r_k, ln_x_w, ln_x_b, g_final):
    B, T, _ = x_prompt.shape
    DB, DS, _ = x_sample.shape
    past_len = page_table.shape[1] * cache_k.shape[2]
    pos_p = jnp.arange(T, dtype=jnp.int32)
    pos_s = past_len + jnp.arange(DS, dtype=jnp.int32)
    xp, xs = x_prompt, x_sample
    kp_l, vp_l, shp_l, wkvp_l = [], [], [], []
    ks_l, vs_l, shs_l, wkvs_l = [], [], [], []
    for l in range(DEPTH):
        lp = dict(g_norm=g_norm[l], w_ada=w_ada[l], b_ada=b_ada[l], ffn_w_gate=ffn_w_gate[l],
                  ffn_w_up=ffn_w_up[l], ffn_w_down=ffn_w_down[l], w_in=w_in[l], w_out=w_out[l],
                  mu_rkv=mu_rkv[l], mu_wag=mu_wag[l], w0=w0[l], w_lora1=w_lora1[l], w_lora2=w_lora2[l],
                  a0=a0[l], a_lora1=a_lora1[l], a_lora2=a_lora2[l], g_lora1=g_lora1[l], g_lora2=g_lora2[l],
                  k_k=k_k[l], k_a=k_a[l], r_k=r_k[l], ln_x_w=ln_x_w[l], ln_x_b=ln_x_b[l])
        h0 = jnp.zeros((B, D_MODEL), x_prompt.dtype)
        S0 = jnp.zeros((B, H_RWKV, HD_RWKV, HD_RWKV), jnp.float32)
        xp, kp, vp, shp, Sp = block(xp, c_prompt, pos_p, h0, S0, lp, moba_prompt)
        k_past = cache_k[l][page_table].reshape(DB, past_len, H_ATT, HD_ATT)
        v_past = cache_v[l][page_table].reshape(DB, past_len, H_ATT, HD_ATT)
        attend_s = functools.partial(moba_sample, k_past=k_past, v_past=v_past)
        xs, ksmp, vsmp, shs, Ss = block(xs, c_sample, pos_s, state_shift[l], state_wkv[l], lp, attend_s)
        kp_l.append(kp); vp_l.append(vp); shp_l.append(shp); wkvp_l.append(Sp)
        ks_l.append(ksmp); vs_l.append(vsmp); shs_l.append(shs); wkvs_l.append(Ss)
    y_prompt = rmsnorm(xp, g_final)
    y_sample = rmsnorm(xs, g_final)
    return (y_prompt, y_sample, jnp.stack(kp_l), jnp.stack(vp_l), jnp.stack(shp_l), jnp.stack(wkvp_l),
            jnp.stack(ks_l), jnp.stack(vs_l), jnp.stack(shs_l), jnp.stack(wkvs_l))
```

```python
import functools

import jax
import jax.numpy as jnp
from jax import lax
from jax.experimental import pallas as pl
from jax.experimental.pallas import tpu as pltpu

F32 = jnp.float32
BF16 = jnp.bfloat16

HD_ATT = 128
HD_RWKV = 64
LANES = 128
MOBA_BLOCK = 256
MOBA_TOPK = 3
ROPE_THETA = 10000.0
RMS_EPS = 1e-6
GN_EPS = 64e-5
NEG_BIG = -1e30
VMEM_LIMIT = 56 * 1024 * 1024


def _cparams(*sem):
    return pltpu.CompilerParams(dimension_semantics=sem, vmem_limit_bytes=VMEM_LIMIT)


def _tile(n, pref, align):
    if n <= pref:
        return n
    t = (pref // align) * align
    while t >= align:
        if n % t == 0:
            return t
        t -= align
    return n


def _dot(a, b):
    return jnp.dot(a, b, preferred_element_type=F32)


def _dot_nt(a, b):
    return lax.dot_general(a, b, (((1,), (1,)), ((), ())), preferred_element_type=F32)


def _ada_kernel(c_ref, w_ref, b_ref, o_ref):
    c = c_ref[...]
    s = (c * jax.nn.sigmoid(c)).astype(BF16)
    o_ref[...] = _dot(s, w_ref[...].astype(BF16)) + b_ref[...]


def _ada(c, w, b):
    m, d = c.shape
    n = w.shape[1]
    tn = _tile(n, 512, LANES)
    return pl.pallas_call(
        _ada_kernel,
        out_shape=jax.ShapeDtypeStruct((m, n), F32),
        grid=(n // tn,),
        in_specs=[pl.BlockSpec((m, d), lambda j: (0, 0)),
                  pl.BlockSpec((d, tn), lambda j: (0, j)),
                  pl.BlockSpec((1, tn), lambda j: (0, j))],
        out_specs=pl.BlockSpec((m, tn), lambda j: (0, j)),
        compiler_params=_cparams("arbitrary"),
        name="ada_proj",
    )(c, w, b.reshape(1, n))


def _modulate_kernel(x_ref, g_ref, sh_ref, sc_ref, *o_refs):
    x = x_ref[...]
    y = x * lax.rsqrt(jnp.mean(x * x, axis=-1, keepdims=True) + RMS_EPS) * g_ref[...]
    h = y * (1.0 + sc_ref[0]) + sh_ref[0]
    o_refs[0][...] = h.astype(BF16)
    if len(o_refs) > 1:
        o_refs[1][...] = h


def _modulate(x, g, sh, sc, rows_per_group, want_f32):
    m, d = x.shape
    r = sh.shape[1]
    tm = m if r > 1 else _tile(rows_per_group, 256, 16)
    grp = (lambda i: (0, 0, 0)) if r > 1 else (lambda i: (i * tm // rows_per_group, 0, 0))
    out_shape = [jax.ShapeDtypeStruct((m, d), BF16)]
    out_specs = [pl.BlockSpec((tm, d), lambda i: (i, 0))]
    if want_f32:
        out_shape.append(jax.ShapeDtypeStruct((m, d), F32))
        out_specs.append(pl.BlockSpec((tm, d), lambda i: (i, 0)))
    return pl.pallas_call(
        _modulate_kernel,
        out_shape=out_shape,
        grid=(m // tm,),
        in_specs=[pl.BlockSpec((tm, d), lambda i: (i, 0)),
                  pl.BlockSpec((1, d), lambda i: (0, 0)),
                  pl.BlockSpec((1, r, d), grp),
                  pl.BlockSpec((1, r, d), grp)],
        out_specs=out_specs,
        compiler_params=_cparams("arbitrary"),
        name="modulate",
    )(x, g.reshape(1, d), sh, sc)


def _rmsnorm_kernel(x_ref, g_ref, o_ref):
    x = x_ref[...]
    o_ref[...] = x * lax.rsqrt(jnp.mean(x * x, axis=-1, keepdims=True) + RMS_EPS) * g_ref[...]


def _rmsnorm(x, g):
    m, d = x.shape
    tm = _tile(m, 256, 8)
    return pl.pallas_call(
        _rmsnorm_kernel,
        out_shape=jax.ShapeDtypeStruct((m, d), F32),
        grid=(m // tm,),
        in_specs=[pl.BlockSpec((tm, d), lambda i: (i, 0)), pl.BlockSpec((1, d), lambda i: (0, 0))],
        out_specs=pl.BlockSpec((tm, d), lambda i: (i, 0)),
        compiler_params=_cparams("arbitrary"),
        name="final_rmsnorm",
    )(x, g.reshape(1, d))


def _ffn_up_kernel(h_ref, wg_ref, wu_ref, o_ref):
    h = h_ref[...]
    g = _dot(h, wg_ref[...])
    u = _dot(h, wu_ref[...])
    o_ref[...] = (g * jax.nn.sigmoid(g) * u).astype(BF16)


def _ffn_up(h, wg, wu):
    m, d = h.shape
    f = wg.shape[1]
    tm = _tile(m, 1024, 16)
    tn = _tile(f, 256, LANES)
    return pl.pallas_call(
        _ffn_up_kernel,
        out_shape=jax.ShapeDtypeStruct((m, f), BF16),
        grid=(m // tm, f // tn),
        in_specs=[pl.BlockSpec((tm, d), lambda i, j: (i, 0)),
                  pl.BlockSpec((d, tn), lambda i, j: (0, j)),
                  pl.BlockSpec((d, tn), lambda i, j: (0, j))],
        out_specs=pl.BlockSpec((tm, tn), lambda i, j: (i, j)),
        compiler_params=_cparams("arbitrary", "arbitrary"),
        name="ffn_up",
    )(h, wg, wu)


def _ffn_down_kernel(u_ref, w_ref, x_ref, ga_ref, o_ref):
    o_ref[...] = x_ref[...] + 0.5 * ga_ref[0] * _dot(u_ref[...], w_ref[...])


def _ffn_down(u, wd, x, ga, rows_per_group):
    m, f = u.shape
    d = wd.shape[1]
    r = ga.shape[1]
    tm = m if r > 1 else _tile(rows_per_group, 512, 16)
    tn = _tile(d, 256, LANES)
    grp = (lambda i, j: (0, 0, j)) if r > 1 else (lambda i, j: (i * tm // rows_per_group, 0, j))
    return pl.pallas_call(
        _ffn_down_kernel,
        out_shape=jax.ShapeDtypeStruct((m, d), F32),
        grid=(m // tm, d // tn),
        in_specs=[pl.BlockSpec((tm, f), lambda i, j: (i, 0)),
                  pl.BlockSpec((f, tn), lambda i, j: (0, j)),
                  pl.BlockSpec((tm, tn), lambda i, j: (i, j)),
                  pl.BlockSpec((1, r, tn), grp)],
        out_specs=pl.BlockSpec((tm, tn), lambda i, j: (i, j)),
        compiler_params=_cparams("arbitrary", "arbitrary"),
        name="ffn_down",
    )(u, wd, x, ga)


def _proj_kernel(h_ref, w_ref, o_ref):
    o_ref[...] = _dot(h_ref[...], w_ref[...])


def _proj_rope_kernel(h_ref, w_ref, cos_ref, sin_ref, o_ref):
    x = _dot(h_ref[...], w_ref[...])
    cos = cos_ref[...]
    sin = sin_ref[...]
    for c in range(x.shape[1] // HD_ATT):
        xc = x[:, c * HD_ATT:(c + 1) * HD_ATT]
        o_ref[:, c * HD_ATT:(c + 1) * HD_ATT] = xc * cos + pltpu.roll(xc, HD_ATT // 2, 1) * sin


def _proj(h, w, col0, ncols, rope=None):
    m, d = h.shape
    tm = _tile(m, 1024, 16)
    tn = _tile(ncols, 256, LANES)
    j0 = col0 // tn
    in_specs = [pl.BlockSpec((tm, d), lambda i, j: (i, 0)),
                pl.BlockSpec((d, tn), lambda i, j: (0, j + j0))]
    args = [h, w]
    kern = _proj_kernel
    if rope is not None:
        in_specs += [pl.BlockSpec((tm, HD_ATT), lambda i, j: (i, 0)),
                     pl.BlockSpec((tm, HD_ATT), lambda i, j: (i, 0))]
        args += list(rope)
        kern = _proj_rope_kernel
    return pl.pallas_call(
        kern,
        out_shape=jax.ShapeDtypeStruct((m, ncols), F32),
        grid=(m // tm, ncols // tn),
        in_specs=in_specs,
        out_specs=pl.BlockSpec((tm, tn), lambda i, j: (i, j)),
        compiler_params=_cparams("arbitrary", "arbitrary"),
        name="in_proj_rope" if rope is not None else "in_proj",
    )(*args)


def _out_proj_kernel(att_ref, rw_ref, wa_ref, wr_ref, x_ref, ga_ref, o_ref):
    y = _dot(att_ref[...], wa_ref[...]) + _dot(rw_ref[...], wr_ref[...])
    o_ref[...] = x_ref[...] + ga_ref[0] * y


def _out_proj(att, rw, w, x, ga, rows_per_group):
    m, da = att.shape
    dr = rw.shape[1]
    d = w.shape[1]
    assert da == dr
    r = ga.shape[1]
    tm = m if r > 1 else _tile(rows_per_group, 1024, 16)
    tn = _tile(d, 256, LANES)
    grp = (lambda i, j: (0, 0, j)) if r > 1 else (lambda i, j: (i * tm // rows_per_group, 0, j))
    return pl.pallas_call(
        _out_proj_kernel,
        out_shape=jax.ShapeDtypeStruct((m, d), F32),
        grid=(m // tm, d // tn),
        in_specs=[pl.BlockSpec((tm, da), lambda i, j: (i, 0)),
                  pl.BlockSpec((tm, dr), lambda i, j: (i, 0)),
                  pl.BlockSpec((da, tn), lambda i, j: (0, j)),
                  pl.BlockSpec((dr, tn), lambda i, j: (1, j)),
                  pl.BlockSpec((tm, tn), lambda i, j: (i, j)),
                  pl.BlockSpec((1, r, tn), grp)],
        out_specs=pl.BlockSpec((tm, tn), lambda i, j: (i, j)),
        compiler_params=_cparams("arbitrary", "arbitrary"),
        name="out_proj",
    )(att, rw, w, w, x, ga)


def _rank_select(g_rows, n_valid, topk):
    nb = len(g_rows)
    sel = []
    for j in range(nb):
        rank = jnp.zeros_like(g_rows[j])
        for n in range(nb):
            if n == j:
                continue
            beats = (g_rows[n] > g_rows[j]) if n > j else (g_rows[n] >= g_rows[j])
            rank = rank + jnp.where(beats, jnp.where(n < n_valid, 1.0, 0.0), 0.0)
        sel.append(rank < float(topk))
    return sel


def _moba_prompt_kernel(q_ref, k_ref, v_ref, o_ref, kb_ref, vt_ref, km_ref, m_ref, l_ref, acc_ref, *, nb, scale):
    i = pl.program_id(2)
    blk = MOBA_BLOCK

    @pl.when(i == 0)
    def _():
        km_ref[...] = jnp.zeros_like(km_ref)
        for j in range(nb):
            kj = k_ref[j * blk:(j + 1) * blk, :]
            km_ref[j:j + 1, :] = jnp.mean(kj, axis=0, keepdims=True)
            kb_ref[j] = kj.astype(BF16)
            vt_ref[j] = jnp.transpose(v_ref[j * blk:(j + 1) * blk, :]).astype(BF16)

    q = q_ref[...]
    qb = q.astype(BF16)
    gt = lax.dot_general(km_ref[...], q, (((1,), (1,)), ((), ())),
                         preferred_element_type=F32, precision=lax.Precision.HIGHEST)
    g_rows = [gt[j:j + 1, :] for j in range(nb)]
    sel = _rank_select(g_rows, i, MOBA_TOPK)

    kpos = lax.broadcasted_iota(jnp.int32, (blk, blk), 0)
    qpos = lax.broadcasted_iota(jnp.int32, (blk, blk), 1)

    s = _dot_nt(kb_ref[i], qb) * scale
    s = jnp.where(kpos <= qpos, s, NEG_BIG)
    m0 = jnp.max(s, axis=0, keepdims=True)
    p = jnp.exp(s - m0)
    m_ref[...] = m0
    l_ref[...] = jnp.sum(p, axis=0, keepdims=True)
    acc_ref[...] = _dot(vt_ref[i], p.astype(BF16))

    for j in range(nb - 1):
        @pl.when(j < i)
        def _(j=j):
            s = _dot_nt(kb_ref[j], qb) * scale
            s = jnp.where(sel[j], s, NEG_BIG)
            m_old = m_ref[...]
            m_new = jnp.maximum(m_old, jnp.max(s, axis=0, keepdims=True))
            alpha = jnp.exp(m_old - m_new)
            p = jnp.exp(s - m_new)
            m_ref[...] = m_new
            l_ref[...] = alpha * l_ref[...] + jnp.sum(p, axis=0, keepdims=True)
            acc_ref[...] = alpha * acc_ref[...] + _dot(vt_ref[j], p.astype(BF16))

    o_ref[...] = jnp.transpose(acc_ref[...] / l_ref[...]).astype(BF16)


def _moba_prompt(q, k, v, batch, seq):
    m, da = q.shape
    nh = da // HD_ATT
    assert seq % MOBA_BLOCK == 0
    nb = seq // MOBA_BLOCK
    nbp = -(-nb // 8) * 8
    kern = functools.partial(_moba_prompt_kernel, nb=nb, scale=HD_ATT ** -0.5)
    return pl.pallas_call(
        kern,
        out_shape=jax.ShapeDtypeStruct((m, da), BF16),
        grid=(batch, nh, nb),
        in_specs=[pl.BlockSpec((MOBA_BLOCK, HD_ATT), lambda b, h, i: (b * nb + i, h)),
                  pl.BlockSpec((seq, HD_ATT), lambda b, h, i: (b, h)),
                  pl.BlockSpec((seq, HD_ATT), lambda b, h, i: (b, h))],
        out_specs=pl.BlockSpec((MOBA_BLOCK, HD_ATT), lambda b, h, i: (b * nb + i, h)),
        scratch_shapes=[pltpu.VMEM((nb, MOBA_BLOCK, HD_ATT), BF16),
                        pltpu.VMEM((nb, HD_ATT, MOBA_BLOCK), BF16),
                        pltpu.VMEM((nbp, HD_ATT), F32),
                        pltpu.VMEM((1, MOBA_BLOCK), F32),
                        pltpu.VMEM((1, MOBA_BLOCK), F32),
                        pltpu.VMEM((HD_ATT, MOBA_BLOCK), F32)],
        compiler_params=_cparams("arbitrary", "arbitrary", "arbitrary"),
        name="moba_prompt",
    )(q, k, v)


def _page_sum_kernel(pt_ref, k_ref, o_ref):
    o_ref[0] = jnp.sum(k_ref[0], axis=0, keepdims=True)


def _page_sums(cache_k2, page_table_flat):
    n_pool, page, da = cache_k2.shape
    n = page_table_flat.shape[0]
    return pl.pallas_call(
        _page_sum_kernel,
        out_shape=jax.ShapeDtypeStruct((n, 1, da), F32),
        grid_spec=pltpu.PrefetchScalarGridSpec(
            num_scalar_prefetch=1,
            grid=(n,),
            in_specs=[pl.BlockSpec((1, page, da), lambda s, pt: (pt[s], 0, 0))],
            out_specs=pl.BlockSpec((1, 1, da), lambda s, pt: (s, 0, 0))),
        compiler_params=_cparams("arbitrary"),
        name="page_key_sums",
    )(page_table_flat, cache_k2)


def _select_kernel(ps_ref, q_ref, o_ref, *, n_blocks, pages_per_block, nh, topk, block_rows):
    ps = ps_ref[0]
    q = q_ref[0]
    rows = []
    for n in range(n_blocks):
        tot = ps[n * pages_per_block:n * pages_per_block + 1, :]
        for e in range(1, pages_per_block):
            tot = tot + ps[n * pages_per_block + e:n * pages_per_block + e + 1, :]
        rows.append(tot / float(block_rows))
    prod = jnp.concatenate(rows, axis=0) * q
    g = jnp.concatenate([jnp.sum(prod[:, h * HD_ATT:(h + 1) * HD_ATT], axis=1, keepdims=True)
                         for h in range(nh)], axis=1)
    idx = lax.broadcasted_iota(jnp.int32, g.shape, 0)
    outs = []
    for _ in range(topk):
        mx = jnp.max(g, axis=0, keepdims=True)
        pick = jnp.min(jnp.where(g == mx, idx, n_blocks), axis=0, keepdims=True)
        outs.append(pick)
        g = jnp.where(idx == pick, -jnp.inf, g)
    o_ref[0] = jnp.concatenate(outs, axis=0)


def _select_blocks(page_sums, q, n_seq, n_pages, page, nh, topk):
    da = q.shape[1]
    ppb = MOBA_BLOCK // page
    n_blocks = n_pages // ppb
    kern = functools.partial(_select_kernel, n_blocks=n_blocks, pages_per_block=ppb, nh=nh, topk=topk,
                             block_rows=MOBA_BLOCK)
    return pl.pallas_call(
        kern,
        out_shape=jax.ShapeDtypeStruct((n_seq, topk, nh), jnp.int32),
        grid=(n_seq,),
        in_specs=[pl.BlockSpec((1, n_pages, da), lambda b: (b, 0, 0)),
                  pl.BlockSpec((1, 1, da), lambda b: (b, 0, 0))],
        out_specs=pl.BlockSpec((1, topk, nh), lambda b: (b, 0, 0)),
        compiler_params=_cparams("arbitrary"),
        name="moba_select",
    )(page_sums.reshape(n_seq, n_pages, da), q.reshape(n_seq, 1, da))


def _moba_sample_kernel(pt_ref, idx_ref, q_ref, kn_ref, vn_ref, k0_ref, k1_ref, v0_ref, v1_ref, o_ref,
                        m_ref, l_ref, acc_ref, *, scale, n_slots):
    s_id = pl.program_id(2)
    q = q_ref[0]

    @pl.when(s_id == 0)
    def _():
        s0 = jnp.sum(q * kn_ref[0], axis=1, keepdims=True) * scale
        m_ref[...] = s0
        l_ref[...] = jnp.ones_like(s0)
        acc_ref[...] = vn_ref[0]

    for k_ref, v_ref in ((k0_ref, v0_ref), (k1_ref, v1_ref)):
        s = jnp.sum(k_ref[0] * q, axis=1, keepdims=True) * scale
        m_old = m_ref[...]
        m_new = jnp.maximum(m_old, jnp.max(s, axis=0, keepdims=True))
        alpha = jnp.exp(m_old - m_new)
        p = jnp.exp(s - m_new)
        m_ref[...] = m_new
        l_ref[...] = alpha * l_ref[...] + jnp.sum(p, axis=0, keepdims=True)
        acc_ref[...] = alpha * acc_ref[...] + jnp.sum(p * v_ref[0], axis=0, keepdims=True)

    @pl.when(s_id == n_slots - 1)
    def _():
        o_ref[0] = (acc_ref[...] / l_ref[...]).astype(BF16)


def _moba_sample(q, k_new, v_new, cache_k2, cache_v2, page_table_flat, sel_idx_flat, n_seq, n_pages, nh, n_slots):
    page = cache_k2.shape[1]
    da = q.shape[1]
    assert MOBA_BLOCK == 2 * page
    kern = functools.partial(_moba_sample_kernel, scale=HD_ATT ** -0.5, n_slots=n_slots)

    def page_map(half):
        def f(b, h, s, pt, idx):
            blk = idx[(b * n_slots + s) * nh + h]
            return (pt[b * n_pages + 2 * blk + half], 0, h)
        return f

    row = lambda b, h, s, pt, idx: (b, 0, h)
    out = pl.pallas_call(
        kern,
        out_shape=jax.ShapeDtypeStruct((n_seq, 1, da), BF16),
        grid_spec=pltpu.PrefetchScalarGridSpec(
            num_scalar_prefetch=2,
            grid=(n_seq, nh, n_slots),
            in_specs=[pl.BlockSpec((1, 1, HD_ATT), row),
                      pl.BlockSpec((1, 1, HD_ATT), row),
                      pl.BlockSpec((1, 1, HD_ATT), row),
                      pl.BlockSpec((1, page, HD_ATT), page_map(0)),
                      pl.BlockSpec((1, page, HD_ATT), page_map(1)),
                      pl.BlockSpec((1, page, HD_ATT), page_map(0)),
                      pl.BlockSpec((1, page, HD_ATT), page_map(1))],
            out_specs=pl.BlockSpec((1, 1, HD_ATT), row),
            scratch_shapes=[pltpu.VMEM((1, 1), F32), pltpu.VMEM((1, 1), F32), pltpu.VMEM((1, HD_ATT), F32)]),
        compiler_params=_cparams("arbitrary", "arbitrary", "arbitrary"),
        name="moba_sample",
    )(page_table_flat, sel_idx_flat, q.reshape(n_seq, 1, da), k_new.reshape(n_seq, 1, da),
      v_new.reshape(n_seq, 1, da), cache_k2, cache_k2, cache_v2, cache_v2)
    return out.reshape(n_seq, da)


def _lora1_kernel(h_ref, hp_ref, mu_ref, w1_ref, a1_ref, g1_ref, tw_ref, ta_ref, tg_ref):
    h = h_ref[...]
    dx = hp_ref[...] - h
    mu = mu_ref[...]
    xw = (h + dx * mu[0:1, :]).astype(BF16)
    xa = (h + dx * mu[1:2, :]).astype(BF16)
    xg = (h + dx * mu[2:3, :]).astype(BF16)
    tw_ref[...] = jnp.tanh(_dot(xw, w1_ref[...])).astype(BF16)
    ta_ref[...] = _dot(xa, a1_ref[...]).astype(BF16)
    tg_ref[...] = jax.nn.sigmoid(_dot(xg, g1_ref[...])).astype(BF16)


def _lora1(h, hprev, mu_wag, w1, a1, g1):
    m, d = h.shape
    tm = _tile(m, 256, 16)
    nw, na, ng = w1.shape[1], a1.shape[1], g1.shape[1]
    row = lambda i: (i, 0)
    full = lambda i: (0, 0)
    return pl.pallas_call(
        _lora1_kernel,
        out_shape=[jax.ShapeDtypeStruct((m, nw), BF16), jax.ShapeDtypeStruct((m, na), BF16),
                   jax.ShapeDtypeStruct((m, ng), BF16)],
        grid=(m // tm,),
        in_specs=[pl.BlockSpec((tm, d), row), pl.BlockSpec((tm, d), row), pl.BlockSpec(mu_wag.shape, full),
                  pl.BlockSpec((d, nw), full), pl.BlockSpec((d, na), full), pl.BlockSpec((d, ng), full)],
        out_specs=[pl.BlockSpec((tm, nw), row), pl.BlockSpec((tm, na), row), pl.BlockSpec((tm, ng), row)],
        compiler_params=_cparams("arbitrary"),
        name="rwkv_lora1",
    )(h, hprev, mu_wag, w1, a1, g1)


def _head_sums(x, bd):
    hi = x.astype(BF16)
    r1 = x - hi.astype(F32)
    mid = r1.astype(BF16)
    lo = (r1 - mid.astype(F32)).astype(BF16)
    return _dot(hi, bd) + _dot(mid, bd) + _dot(lo, bd)


def _block_diag_ones():
    a = lax.broadcasted_iota(jnp.int32, (LANES, LANES), 0) // HD_RWKV
    b = lax.broadcasted_iota(jnp.int32, (LANES, LANES), 1) // HD_RWKV
    return jnp.where(a == b, 1.0, 0.0).astype(BF16)


def _rwkv_prep_kernel(p_ref, pp_ref, tw_ref, ta_ref, tg_ref, w2_ref, a2_ref, g2_ref, mu_ref, vec_ref,
                      r_ref, dec_ref, km_ref, v_ref, kk_ref, be_ref, g_ref, bo_ref):
    dr = r_ref.shape[1]
    bd = _block_diag_ones()
    mu = mu_ref[...]
    vec = vec_ref[...]
    w0, a0, k_k, k_a, r_k = vec[0:1, :], vec[1:2, :], vec[2:3, :], vec[3:4, :], vec[4:5, :]

    def shifted(c):
        pc = p_ref[:, c * dr:(c + 1) * dr]
        return pc + (pp_ref[:, c * dr:(c + 1) * dr] - pc) * mu[c:c + 1, :]

    r = shifted(0)
    k = shifted(1)
    v = shifted(2)
    w = -jnp.logaddexp(-(w0 + _dot(tw_ref[...], w2_ref[...])), 0.0) - 0.5
    dec = jnp.exp(-jnp.exp(w))
    a = jax.nn.sigmoid(a0 + _dot(ta_ref[...], a2_ref[...]))
    g = _dot(tg_ref[...], g2_ref[...])
    kk = k * k_k
    kmod = k * (1.0 + (a - 1.0) * k_a)
    rk = r * kmod * r_k
    for c in range(dr // LANES):
        sl = slice(c * LANES, (c + 1) * LANES)
        kkc = kk[:, sl]
        kkc = kkc * lax.rsqrt(jnp.maximum(_head_sums(kkc * kkc, bd), 1e-24))
        kk_ref[:, sl] = kkc
        be_ref[:, sl] = kkc * a[:, sl]
        bo_ref[:, sl] = _head_sums(rk[:, sl], bd) * v[:, sl]
    r_ref[...] = r
    dec_ref[...] = dec
    km_ref[...] = kmod
    v_ref[...] = v
    g_ref[...] = g


def _rwkv_prep(p, pprev, tw, ta, tg, w2, a2, g2, mu_rkv, vecs):
    m = p.shape[0]
    dr = p.shape[1] // 3
    tm = _tile(m, 128, 16)
    row = lambda i: (i, 0)
    full = lambda i: (0, 0)
    outs = [jax.ShapeDtypeStruct((m, dr), F32)] * 8
    return pl.pallas_call(
        _rwkv_prep_kernel,
        out_shape=outs,
        grid=(m // tm,),
        in_specs=[pl.BlockSpec((tm, 3 * dr), row), pl.BlockSpec((tm, 3 * dr), row),
                  pl.BlockSpec((tm, tw.shape[1]), row), pl.BlockSpec((tm, ta.shape[1]), row),
                  pl.BlockSpec((tm, tg.shape[1]), row),
                  pl.BlockSpec(w2.shape, full), pl.BlockSpec(a2.shape, full), pl.BlockSpec(g2.shape, full),
                  pl.BlockSpec(mu_rkv.shape, full), pl.BlockSpec(vecs.shape, full)],
        out_specs=[pl.BlockSpec((tm, dr), row)] * 8,
        compiler_params=_cparams("arbitrary"),
        name="rwkv_prep",
    )(p, pprev, tw, ta, tg, w2, a2, g2, mu_rkv, vecs)


def _rwkv_scan_kernel(kk_ref, dec_ref, be_ref, km_ref, r_ref, v_ref, s0_ref, y_ref, s_ref, *, tc, npair):
    c = pl.program_id(1)
    n = HD_RWKV

    @pl.when(c == 0)
    def _():
        s_ref[...] = s0_ref[...]

    bd = _block_diag_ones()
    lane = lax.broadcasted_iota(jnp.int32, (n, LANES), 1)
    sub = lax.broadcasted_iota(jnp.int32, (n, LANES), 0)
    diag = (lane % n) == sub

    def step(t, carry):
        s_old, xs, vs = [], [], []
        for p in range(npair):
            sp = s_ref[0, p]
            s_old.append(sp)
            xs.append(sp * kk_ref[t, pl.ds(p, 1), :])
            vs.append(jnp.where(diag, v_ref[t, pl.ds(p, 1), :], 0.0))
        sa_all = _dot(jnp.concatenate(xs, axis=0).astype(BF16), bd)
        vb_all = _dot(jnp.concatenate(vs, axis=0).astype(BF16), bd)
        ys = []
        for p in range(npair):
            sa = sa_all[p * n:(p + 1) * n, :]
            vb = vb_all[p * n:(p + 1) * n, :]
            s_new = (s_old[p] * dec_ref[t, pl.ds(p, 1), :] - sa * be_ref[t, pl.ds(p, 1), :]
                     + vb * km_ref[t, pl.ds(p, 1), :])
            s_ref[0, p] = s_new
            ys.append(s_new * r_ref[t, pl.ds(p, 1), :])
        y_all = _dot(jnp.concatenate(ys, axis=0).astype(BF16), bd)
        for p in range(npair):
            yb = y_all[p * n:(p + 1) * n, :]
            y_ref[t, pl.ds(p, 1), :] = jnp.sum(jnp.where(diag, yb, 0.0), axis=0, keepdims=True)
        return carry

    lax.fori_loop(0, tc, step, 0)


def _rwkv_scan(kk, dec, be, km, r, v, s0, batch, seq):
    m, npair, _ = kk.shape
    tc = _tile(seq, 128, 8) if seq > 1 else 1
    nc = seq // tc
    kern = functools.partial(_rwkv_scan_kernel, tc=tc, npair=npair)
    op = pl.BlockSpec((tc, npair, LANES), lambda b, c: (b * nc + c, 0, 0))
    st = pl.BlockSpec((1, npair, HD_RWKV, LANES), lambda b, c: (b, 0, 0, 0))
    return pl.pallas_call(
        kern,
        out_shape=[jax.ShapeDtypeStruct((m, npair, LANES), F32),
                   jax.ShapeDtypeStruct((batch, npair, HD_RWKV, LANES), F32)],
        grid=(batch, nc),
        in_specs=[op] * 6 + [st],
        out_specs=[op, st],
        compiler_params=_cparams("arbitrary", "arbitrary"),
        name="rwkv_scan",
    )(kk, dec, be, km, r, v, s0)


def _rwkv_post_kernel(y_ref, bo_ref, g_ref, ln_ref, o_ref):
    bd = _block_diag_ones()
    ln = ln_ref[...]
    inv_n = 1.0 / HD_RWKV
    for c in range(y_ref.shape[1] // LANES):
        sl = slice(c * LANES, (c + 1) * LANES)
        y = y_ref[:, sl]
        mean = _head_sums(y, bd) * inv_n
        d = y - mean
        var = _head_sums(d * d, bd) * inv_n
        yn = d * lax.rsqrt(var + GN_EPS) * ln[0:1, sl] + ln[1:2, sl]
        o_ref[:, sl] = ((yn + bo_ref[:, sl]) * g_ref[:, sl]).astype(BF16)


def _rwkv_post(y, bonus, g, ln):
    m, dr = y.shape
    tm = _tile(m, 256, 16)
    row = lambda i: (i, 0)
    return pl.pallas_call(
        _rwkv_post_kernel,
        out_shape=jax.ShapeDtypeStruct((m, dr), BF16),
        grid=(m // tm,),
        in_specs=[pl.BlockSpec((tm, dr), row)] * 3 + [pl.BlockSpec(ln.shape, lambda i: (0, 0))],
        out_specs=pl.BlockSpec((tm, dr), row),
        compiler_params=_cparams("arbitrary"),
        name="rwkv_post",
    )(y, bonus, g, ln)


def _pack_state(s):
    b, h, n, _ = s.shape
    return s.reshape(b, h // 2, 2, n, n).transpose(0, 1, 3, 2, 4).reshape(b, h // 2, n, 2 * n)


def _unpack_state(s):
    b, p, n, _ = s.shape
    return s.reshape(b, p, n, 2, n).transpose(0, 1, 3, 2, 4).reshape(b, 2 * p, n, n)


def _pad_cols(w, n):
    return jnp.pad(w, ((0, 0), (0, n - w.shape[1])))


def _pad_rows(w, n):
    return jnp.pad(w, ((0, n - w.shape[0]), (0, 0)))


def _rope_tables(pos, rows_per_pos):
    half = HD_ATT // 2
    inv = ROPE_THETA ** (-jnp.arange(half, dtype=F32) / half)
    ang = pos.astype(F32)[:, None] * inv[None, :]
    cos = jnp.cos(ang)
    sin = jnp.sin(ang)
    cos2 = jnp.concatenate([cos, cos], axis=-1)
    sin2 = jnp.concatenate([-sin, sin], axis=-1)
    if rows_per_pos is not None:
        cos2 = jnp.broadcast_to(cos2, (rows_per_pos, HD_ATT))
        sin2 = jnp.broadcast_to(sin2, (rows_per_pos, HD_ATT))
    return cos2, sin2


def kernel(x_prompt, x_sample, cache_k, cache_v, state_shift, state_wkv, page_table, c_prompt, c_sample, g_norm, w_ada, b_ada, ffn_w_gate, ffn_w_up, ffn_w_down, w_in, w_out, mu_rkv, mu_wag, w0, w_lora1, w_lora2, a0, a_lora1, a_lora2, g_lora1, g_lora2, k_k, k_a, r_k, ln_x_w, ln_x_b, g_final):
    B, T, D = x_prompt.shape
    DB, DS, _ = x_sample.shape
    depth = g_norm.shape[0]
    assert DS == 1
    n_pages = page_table.shape[1]
    page = cache_k.shape[2]
    past_len = n_pages * page
    assert past_len % MOBA_BLOCK == 0 and MOBA_BLOCK == 2 * page
    da = cache_k.shape[3] * cache_k.shape[4]
    nh = da // HD_ATT
    dr = D - da
    n_slots = min(MOBA_TOPK, past_len // MOBA_BLOCK)

    xp = x_prompt.reshape(B * T, D)
    xs = x_sample.reshape(DB, D)
    pt_flat = page_table.reshape(-1).astype(jnp.int32)
    cos_p, sin_p = _rope_tables(jnp.arange(T, dtype=jnp.int32), None)
    cos_p = jnp.tile(cos_p, (B, 1))
    sin_p = jnp.tile(sin_p, (B, 1))
    cos_s, sin_s = _rope_tables(jnp.full((1,), past_len, dtype=jnp.int32), DB)

    outs = [[] for _ in range(8)]
    n_c = B + DB
    n_cp = -(-n_c // 8) * 8
    c_all = jnp.pad(jnp.concatenate([c_prompt, c_sample], axis=0), ((0, n_cp - n_c), (0, 0)))

    for l in range(depth):
        bf = lambda w: w.astype(BF16)
        wg, wu, wd = bf(ffn_w_gate[l]), bf(ffn_w_up[l]), bf(ffn_w_down[l])
        win, wout = bf(w_in[l]), bf(w_out[l])
        lw = -(-w_lora1.shape[2] // LANES) * LANES
        la = -(-a_lora1.shape[2] // LANES) * LANES
        lg = -(-g_lora1.shape[2] // LANES) * LANES
        w1, a1, g1 = bf(_pad_cols(w_lora1[l], lw)), bf(_pad_cols(a_lora1[l], la)), bf(_pad_cols(g_lora1[l], lg))
        w2, a2, g2 = bf(_pad_rows(w_lora2[l], lw)), bf(_pad_rows(a_lora2[l], la)), bf(_pad_rows(g_lora2[l], lg))
        vecs = jnp.stack([w0[l], a0[l], k_k[l], k_a[l], r_k[l].reshape(dr)], axis=0)
        vecs = jnp.pad(vecs, ((0, 3), (0, 0)))
        ln = jnp.pad(jnp.stack([ln_x_w[l], ln_x_b[l]], axis=0), ((0, 6), (0, 0)))
        mu3 = jnp.pad(mu_rkv[l], ((0, 5), (0, 0)))
        muw = jnp.pad(mu_wag[l], ((0, 5), (0, 0)))

        mod = _ada(c_all, w_ada[l], b_ada[l])
        chunks = [mod[:, k * D:(k + 1) * D] for k in range(9)]
        mod_p = [ch[:B].reshape(B, 1, D) for ch in chunks]
        mod_s = [ch[B:B + DB].reshape(1, DB, D) for ch in chunks]

        def ffn(x, mods, rpg, idx):
            sub = 2 * idx
            sh, sc, ga = mods[3 * sub], mods[3 * sub + 1], mods[3 * sub + 2]
            (h,) = _modulate(x, g_norm[l, sub], sh, sc, rpg, False)
            u = _ffn_up(h, wg[idx], wu[idx])
            return _ffn_down(u, wd[idx], x, ga, rpg)

        xp = ffn(xp, mod_p, T, 0)
        hb, hf = _modulate(xp, g_norm[l, 1], mod_p[3], mod_p[4], T, True)
        q = _proj(hb, win, 0, da, (cos_p, sin_p))
        k = _proj(hb, win, da, da, (cos_p, sin_p))
        v = _proj(hb, win, 2 * da, da)
        prk = _proj(hb, win, 3 * da, 3 * dr)
        att = _moba_prompt(q, k, v, B, T)

        xs = ffn(xs, mod_s, 1, 0)
        hbs, hfs = _modulate(xs, g_norm[l, 1], mod_s[3], mod_s[4], 1, True)
        qs = _proj(hbs, win, 0, da, (cos_s, sin_s))
        ks = _proj(hbs, win, da, da, (cos_s, sin_s))
        vs = _proj(hbs, win, 2 * da, da)
        n_prev = DB + B
        n_prev_p = -(-n_prev // 16) * 16
        prev_rows = jnp.concatenate([state_shift[l], jnp.zeros((B, D), F32),
                                     jnp.zeros((n_prev_p - n_prev, D), F32)], axis=0)
        p_first = _proj(bf(prev_rows), win, 3 * da, 3 * dr)
        prk_s = _proj(hbs, win, 3 * da, 3 * dr)

        h3 = hf.reshape(B, T, D)
        hprev = jnp.concatenate([jnp.zeros((B, 1, D), F32), h3[:, :-1]], axis=1).reshape(B * T, D)
        p3 = prk.reshape(B, T, 3 * dr)
        pprev = jnp.concatenate([p_first[DB:DB + B][:, None], p3[:, :-1]], axis=1).reshape(B * T, 3 * dr)

        def rwkv(hf_, hprev_, prk_, pprev_, s0, nb_, nt_):
            tw, ta, tg = _lora1(hf_, hprev_, muw, w1, a1, g1)
            r_, dec_, km_, v_, kk_, be_, g_, bo_ = _rwkv_prep(prk_, pprev_, tw, ta, tg, w2, a2, g2, mu3, vecs)
            npair = dr // LANES
            as3 = lambda a: a.reshape(a.shape[0], npair, LANES)
            y, s_fin = _rwkv_scan(as3(kk_), as3(dec_), as3(be_), as3(km_), as3(r_), as3(v_), s0, nb_, nt_)
            rw = _rwkv_post(y.reshape(-1, dr), bo_, g_, ln)
            return rw, s_fin

        s0_p = jnp.zeros((B, dr // LANES, HD_RWKV, LANES), F32)
        rw_p, sfin_p = rwkv(hf, hprev, prk, pprev, s0_p, B, T)
        xp = _out_proj(att, rw_p, wout, xp, mod_p[5], T)
        xp = ffn(xp, mod_p, T, 1)

        ck2 = cache_k[l].reshape(cache_k.shape[1], page, da)
        cv2 = cache_v[l].reshape(cache_v.shape[1], page, da)
        psums = _page_sums(ck2, pt_flat)
        sel = _select_blocks(psums, qs, DB, n_pages, page, nh, n_slots)
        att_s = _moba_sample(qs, ks, vs, ck2, cv2, pt_flat, sel.reshape(-1), DB, n_pages, nh, n_slots)

        rw_s, sfin_s = rwkv(hfs, state_shift[l], prk_s, p_first[:DB], _pack_state(state_wkv[l]), DB, 1)
        xs = _out_proj(att_s, rw_s, wout, xs, mod_s[5], 1)
        xs = ffn(xs, mod_s, 1, 1)

        outs[0].append(k.reshape(B, T, nh, HD_ATT))
        outs[1].append(v.reshape(B, T, nh, HD_ATT))
        outs[2].append(h3[:, -1])
        outs[3].append(_unpack_state(sfin_p))
        outs[4].append(ks.reshape(DB, 1, nh, HD_ATT))
        outs[5].append(vs.reshape(DB, 1, nh, HD_ATT))
        outs[6].append(hfs)
        outs[7].append(_unpack_state(sfin_s))

    y_prompt = _rmsnorm(xp, g_final).reshape(B, T, D)
    y_sample = _rmsnorm(xs, g_final).reshape(DB, 1, D)
    return (y_prompt, y_sample) + tuple(jnp.stack(o) for o in outs)
```

```python
import functools

import jax
import jax.numpy as jnp
from jax import lax
from jax.experimental import pallas as pl
from jax.experimental.pallas import tpu as pltpu

F32 = jnp.float32
BF16 = jnp.bfloat16

HD_ATT = 128
HD_RWKV = 64
LANES = 128
MOBA_BLOCK = 256
MOBA_TOPK = 3
ROPE_THETA = 10000.0
RMS_EPS = 1e-6
GN_EPS = 64e-5
NEG_BIG = -1e30
VMEM_LIMIT = 56 * 1024 * 1024


def _cparams(*sem):
    return pltpu.CompilerParams(dimension_semantics=sem, vmem_limit_bytes=VMEM_LIMIT)


def _tile(n, pref, align):
    if n <= pref:
        return n
    t = (pref // align) * align
    while t >= align:
        if n % t == 0:
            return t
        t -= align
    return n


def _dot(a, b):
    return jnp.dot(a, b, preferred_element_type=F32)


def _dot_nt(a, b):
    return lax.dot_general(a, b, (((1,), (1,)), ((), ())), preferred_element_type=F32)


def _ada_kernel(c_ref, w_ref, b_ref, o_ref):
    c = c_ref[...]
    s = (c * jax.nn.sigmoid(c)).astype(BF16)
    o_ref[...] = _dot(s, w_ref[...].astype(BF16)) + b_ref[...]


def _ada(c, w, b):
    m, d = c.shape
    n = w.shape[1]
    tn = _tile(n, 512, LANES)
    return pl.pallas_call(
        _ada_kernel,
        out_shape=jax.ShapeDtypeStruct((m, n), F32),
        grid=(n // tn,),
        in_specs=[pl.BlockSpec((m, d), lambda j: (0, 0)),
                  pl.BlockSpec((d, tn), lambda j: (0, j)),
                  pl.BlockSpec((1, tn), lambda j: (0, j))],
        out_specs=pl.BlockSpec((m, tn), lambda j: (0, j)),
        compiler_params=_cparams("arbitrary"),
        name="ada_proj",
    )(c, w, b.reshape(1, n))


def _modulate_kernel(x_ref, g_ref, sh_ref, sc_ref, *o_refs):
    x = x_ref[...]
    y = x * lax.rsqrt(jnp.mean(x * x, axis=-1, keepdims=True) + RMS_EPS) * g_ref[...]
    h = y * (1.0 + sc_ref[0]) + sh_ref[0]
    o_refs[0][...] = h.astype(BF16)
    if len(o_refs) > 1:
        o_refs[1][...] = h


def _modulate(x, g, sh, sc, rows_per_group, want_f32):
    m, d = x.shape
    r = sh.shape[1]
    tm = m if r > 1 else _tile(rows_per_group, 256, 16)
    grp = (lambda i: (0, 0, 0)) if r > 1 else (lambda i: (i * tm // rows_per_group, 0, 0))
    out_shape = [jax.ShapeDtypeStruct((m, d), BF16)]
    out_specs = [pl.BlockSpec((tm, d), lambda i: (i, 0))]
    if want_f32:
        out_shape.append(jax.ShapeDtypeStruct((m, d), F32))
        out_specs.append(pl.BlockSpec((tm, d), lambda i: (i, 0)))
    return pl.pallas_call(
        _modulate_kernel,
        out_shape=out_shape,
        grid=(m // tm,),
        in_specs=[pl.BlockSpec((tm, d), lambda i: (i, 0)),
                  pl.BlockSpec((1, d), lambda i: (0, 0)),
                  pl.BlockSpec((1, r, d), grp),
                  pl.BlockSpec((1, r, d), grp)],
        out_specs=out_specs,
        compiler_params=_cparams("arbitrary"),
        name="modulate",
    )(x, g.reshape(1, d), sh, sc)


def _rmsnorm_kernel(x_ref, g_ref, o_ref):
    x = x_ref[...]
    o_ref[...] = x * lax.rsqrt(jnp.mean(x * x, axis=-1, keepdims=True) + RMS_EPS) * g_ref[...]


def _rmsnorm(x, g):
    m, d = x.shape
    tm = _tile(m, 256, 8)
    return pl.pallas_call(
        _rmsnorm_kernel,
        out_shape=jax.ShapeDtypeStruct((m, d), F32),
        grid=(m // tm,),
        in_specs=[pl.BlockSpec((tm, d), lambda i: (i, 0)), pl.BlockSpec((1, d), lambda i: (0, 0))],
        out_specs=pl.BlockSpec((tm, d), lambda i: (i, 0)),
        compiler_params=_cparams("arbitrary"),
        name="final_rmsnorm",
    )(x, g.reshape(1, d))


def _ffn_up_kernel(h_ref, wg_ref, wu_ref, o_ref):
    h = h_ref[...]
    g = _dot(h, wg_ref[...].astype(BF16))
    u = _dot(h, wu_ref[...].astype(BF16))
    o_ref[...] = (g * jax.nn.sigmoid(g) * u).astype(BF16)


def _ffn_up(h, wg, wu):
    m, d = h.shape
    f = wg.shape[1]
    tm = _tile(m, 1024, 16)
    tn = _tile(f, 256, LANES)
    return pl.pallas_call(
        _ffn_up_kernel,
        out_shape=jax.ShapeDtypeStruct((m, f), BF16),
        grid=(m // tm, f // tn),
        in_specs=[pl.BlockSpec((tm, d), lambda i, j: (i, 0)),
                  pl.BlockSpec((d, tn), lambda i, j: (0, j)),
                  pl.BlockSpec((d, tn), lambda i, j: (0, j))],
        out_specs=pl.BlockSpec((tm, tn), lambda i, j: (i, j)),
        compiler_params=_cparams("arbitrary", "arbitrary"),
        name="ffn_up",
    )(h, wg, wu)


def _ffn_down_kernel(u_ref, w_ref, x_ref, ga_ref, o_ref):
    o_ref[...] = x_ref[...] + 0.5 * ga_ref[0] * _dot(u_ref[...], w_ref[...])


def _ffn_down(u, wd, x, ga, rows_per_group):
    m, f = u.shape
    d = wd.shape[1]
    r = ga.shape[1]
    tm = m if r > 1 else _tile(rows_per_group, 512, 16)
    tn = _tile(d, 256, LANES)
    grp = (lambda i, j: (0, 0, j)) if r > 1 else (lambda i, j: (i * tm // rows_per_group, 0, j))
    return pl.pallas_call(
        _ffn_down_kernel,
        out_shape=jax.ShapeDtypeStruct((m, d), F32),
        grid=(m // tm, d // tn),
        in_specs=[pl.BlockSpec((tm, f), lambda i, j: (i, 0)),
                  pl.BlockSpec((f, tn), lambda i, j: (0, j)),
                  pl.BlockSpec((tm, tn), lambda i, j: (i, j)),
                  pl.BlockSpec((1, r, tn), grp)],
        out_specs=pl.BlockSpec((tm, tn), lambda i, j: (i, j)),
        compiler_params=_cparams("arbitrary", "arbitrary"),
        name="ffn_down",
    )(u, wd, x, ga)


def _proj_kernel(h_ref, w_ref, o_ref):
    o_ref[...] = _dot(h_ref[...], w_ref[...].astype(BF16))


def _proj_rope_kernel(h_ref, w_ref, cos_ref, sin_ref, o_ref):
    x = _dot(h_ref[...], w_ref[...].astype(BF16))
    cos = cos_ref[...]
    sin = sin_ref[...]
    for c in range(x.shape[1] // HD_ATT):
        xc = x[:, c * HD_ATT:(c + 1) * HD_ATT]
        o_ref[:, c * HD_ATT:(c + 1) * HD_ATT] = xc * cos + pltpu.roll(xc, HD_ATT // 2, 1) * sin


def _proj(h, w, col0, ncols, rope=None):
    m, d = h.shape
    tm = _tile(m, 1024, 16)
    tn = _tile(ncols, 256, LANES)
    j0 = col0 // tn
    in_specs = [pl.BlockSpec((tm, d), lambda i, j: (i, 0)),
                pl.BlockSpec((d, tn), lambda i, j: (0, j + j0))]
    args = [h, w]
    kern = _proj_kernel
    if rope is not None:
        in_specs += [pl.BlockSpec((tm, HD_ATT), lambda i, j: (i, 0)),
                     pl.BlockSpec((tm, HD_ATT), lambda i, j: (i, 0))]
        args += list(rope)
        kern = _proj_rope_kernel
    return pl.pallas_call(
        kern,
        out_shape=jax.ShapeDtypeStruct((m, ncols), F32),
        grid=(m // tm, ncols // tn),
        in_specs=in_specs,
        out_specs=pl.BlockSpec((tm, tn), lambda i, j: (i, j)),
        compiler_params=_cparams("arbitrary", "arbitrary"),
        name="in_proj_rope" if rope is not None else "in_proj",
    )(*args)


def _out_proj_kernel(att_ref, rw_ref, wa_ref, wr_ref, x_ref, ga_ref, o_ref):
    y = _dot(att_ref[...], wa_ref[...]) + _dot(rw_ref[...], wr_ref[...])
    o_ref[...] = x_ref[...] + ga_ref[0] * y


def _out_proj(att, rw, w, x, ga, rows_per_group):
    m, da = att.shape
    dr = rw.shape[1]
    d = w.shape[1]
    assert da == dr
    r = ga.shape[1]
    tm = m if r > 1 else _tile(rows_per_group, 1024, 16)
    tn = _tile(d, 256, LANES)
    grp = (lambda i, j: (0, 0, j)) if r > 1 else (lambda i, j: (i * tm // rows_per_group, 0, j))
    return pl.pallas_call(
        _out_proj_kernel,
        out_shape=jax.ShapeDtypeStruct((m, d), F32),
        grid=(m // tm, d // tn),
        in_specs=[pl.BlockSpec((tm, da), lambda i, j: (i, 0)),
                  pl.BlockSpec((tm, dr), lambda i, j: (i, 0)),
                  pl.BlockSpec((da, tn), lambda i, j: (0, j)),
                  pl.BlockSpec((dr, tn), lambda i, j: (1, j)),
                  pl.BlockSpec((tm, tn), lambda i, j: (i, j)),
                  pl.BlockSpec((1, r, tn), grp)],
        out_specs=pl.BlockSpec((tm, tn), lambda i, j: (i, j)),
        compiler_params=_cparams("arbitrary", "arbitrary"),
        name="out_proj",
    )(att, rw, w, w, x, ga)


def _rank_select(g_rows, n_valid, topk):
    nb = len(g_rows)
    sel = []
    for j in range(nb):
        rank = jnp.zeros_like(g_rows[j])
        for n in range(nb):
            if n == j:
                continue
            beats = (g_rows[n] > g_rows[j]) if n > j else (g_rows[n] >= g_rows[j])
            rank = rank + jnp.where(beats, jnp.where(n < n_valid, 1.0, 0.0), 0.0)
        sel.append(rank < float(topk))
    return sel


def _moba_prompt_kernel(q_ref, k_ref, v_ref, o_ref, kb_ref, vt_ref, km_ref, m_ref, l_ref, acc_ref, *, nb, scale):
    i = pl.program_id(2)
    blk = MOBA_BLOCK

    @pl.when(i == 0)
    def _():
        km_ref[...] = jnp.zeros_like(km_ref)
        for j in range(nb):
            kj = k_ref[j * blk:(j + 1) * blk, :]
            km_ref[j:j + 1, :] = jnp.mean(kj, axis=0, keepdims=True)
            kb_ref[j] = kj.astype(BF16)
            vt_ref[j] = jnp.transpose(v_ref[j * blk:(j + 1) * blk, :]).astype(BF16)

    q = q_ref[...]
    qb = q.astype(BF16)
    gt = lax.dot_general(km_ref[...], q, (((1,), (1,)), ((), ())),
                         preferred_element_type=F32, precision=lax.Precision.HIGHEST)
    g_rows = [gt[j:j + 1, :] for j in range(nb)]
    sel = _rank_select(g_rows, i, MOBA_TOPK)

    kpos = lax.broadcasted_iota(jnp.int32, (blk, blk), 0)
    qpos = lax.broadcasted_iota(jnp.int32, (blk, blk), 1)

    s = _dot_nt(kb_ref[i], qb) * scale
    s = jnp.where(kpos <= qpos, s, NEG_BIG)
    m0 = jnp.max(s, axis=0, keepdims=True)
    p = jnp.exp(s - m0)
    m_ref[...] = m0
    l_ref[...] = jnp.sum(p, axis=0, keepdims=True)
    acc_ref[...] = _dot(vt_ref[i], p.astype(BF16))

    for j in range(nb - 1):
        @pl.when(j < i)
        def _(j=j):
            s = _dot_nt(kb_ref[j], qb) * scale
            s = jnp.where(sel[j], s, NEG_BIG)
            m_old = m_ref[...]
            m_new = jnp.maximum(m_old, jnp.max(s, axis=0, keepdims=True))
            alpha = jnp.exp(m_old - m_new)
            p = jnp.exp(s - m_new)
            m_ref[...] = m_new
            l_ref[...] = alpha * l_ref[...] + jnp.sum(p, axis=0, keepdims=True)
            acc_ref[...] = alpha * acc_ref[...] + _dot(vt_ref[j], p.astype(BF16))

    o_ref[...] = jnp.transpose(acc_ref[...] / l_ref[...]).astype(BF16)


def _moba_prompt(q, k, v, batch, seq):
    m, da = q.shape
    nh = da // HD_ATT
    assert seq % MOBA_BLOCK == 0
    nb = seq // MOBA_BLOCK
    nbp = -(-nb // 8) * 8
    kern = functools.partial(_moba_prompt_kernel, nb=nb, scale=HD_ATT ** -0.5)
    return pl.pallas_call(
        kern,
        out_shape=jax.ShapeDtypeStruct((m, da), BF16),
        grid=(batch, nh, nb),
        in_specs=[pl.BlockSpec((MOBA_BLOCK, HD_ATT), lambda b, h, i: (b * nb + i, h)),
                  pl.BlockSpec((seq, HD_ATT), lambda b, h, i: (b, h)),
                  pl.BlockSpec((seq, HD_ATT), lambda b, h, i: (b, h))],
        out_specs=pl.BlockSpec((MOBA_BLOCK, HD_ATT), lambda b, h, i: (b * nb + i, h)),
        scratch_shapes=[pltpu.VMEM((nb, MOBA_BLOCK, HD_ATT), BF16),
                        pltpu.VMEM((nb, HD_ATT, MOBA_BLOCK), BF16),
                        pltpu.VMEM((nbp, HD_ATT), F32),
                        pltpu.VMEM((1, MOBA_BLOCK), F32),
                        pltpu.VMEM((1, MOBA_BLOCK), F32),
                        pltpu.VMEM((HD_ATT, MOBA_BLOCK), F32)],
        compiler_params=_cparams("arbitrary", "arbitrary", "arbitrary"),
        name="moba_prompt",
    )(q, k, v)


def _block_mean_kernel(pt_ref, k0_ref, k1_ref, o_ref):
    tot = jnp.sum(k0_ref[...], axis=0) + jnp.sum(k1_ref[...], axis=0)
    o_ref[0] = tot * (1.0 / MOBA_BLOCK)


def _block_means(cache_k, layer, page_table_flat):
    _, _, page, nh, hd = cache_k.shape
    n = page_table_flat.shape[0] // 2

    def page_map(half):
        return lambda s, pt: (layer, pt[2 * s + half], 0, 0, 0)

    return pl.pallas_call(
        _block_mean_kernel,
        out_shape=jax.ShapeDtypeStruct((n, nh, hd), F32),
        grid_spec=pltpu.PrefetchScalarGridSpec(
            num_scalar_prefetch=1,
            grid=(n,),
            in_specs=[pl.BlockSpec((None, None, page, nh, hd), page_map(0)),
                      pl.BlockSpec((None, None, page, nh, hd), page_map(1))],
            out_specs=pl.BlockSpec((1, nh, hd), lambda s, pt: (s, 0, 0))),
        compiler_params=_cparams("arbitrary"),
        name="block_key_means",
    )(page_table_flat, cache_k, cache_k)


def _select_kernel(km_ref, q_ref, o_ref, *, n_blocks, topk):
    q = q_ref[0]
    g = [jnp.sum(km_ref[0, n] * q, axis=1, keepdims=True) for n in range(n_blocks)]
    for s in range(topk):
        mx = g[0]
        for n in range(1, n_blocks):
            mx = jnp.maximum(mx, g[n])
        pick = jnp.full(mx.shape, n_blocks, jnp.int32)
        for n in range(n_blocks - 1, -1, -1):
            pick = jnp.where(g[n] == mx, n, pick)
        o_ref[0, s] = pick
        g = [jnp.where(pick == n, -jnp.inf, g[n]) for n in range(n_blocks)]


def _select_blocks(kmeans, q, n_seq, n_blocks, nh, topk):
    hd = HD_ATT
    kern = functools.partial(_select_kernel, n_blocks=n_blocks, topk=topk)
    return pl.pallas_call(
        kern,
        out_shape=jax.ShapeDtypeStruct((n_seq, topk, nh, 1), jnp.int32),
        grid=(n_seq,),
        in_specs=[pl.BlockSpec((1, n_blocks, nh, hd), lambda b: (b, 0, 0, 0)),
                  pl.BlockSpec((1, nh, hd), lambda b: (b, 0, 0))],
        out_specs=pl.BlockSpec((1, topk, nh, 1), lambda b: (b, 0, 0, 0)),
        compiler_params=_cparams("arbitrary"),
        name="moba_select",
    )(kmeans.reshape(n_seq, n_blocks, nh, hd), q.reshape(n_seq, nh, hd))


def _moba_sample_kernel(pt_ref, idx_ref, q_ref, kn_ref, vn_ref, ck_hbm, cv_hbm, o_ref, kbuf, vbuf, sem,
                        *, layer, scale, n_slots, n_pages, nh, page, n_seq):
    b = pl.program_id(0)
    n_pg = 2 * n_slots

    def copies(seq, slot):
        out = []
        for h in range(nh):
            for s in range(n_slots):
                blk = idx_ref[(seq * n_slots + s) * nh + h]
                for half in range(2):
                    pid = pt_ref[seq * n_pages + 2 * blk + half]
                    j = s * 2 + half
                    out.append(pltpu.make_async_copy(ck_hbm.at[layer, pid, :, h, :], kbuf.at[slot, h, j],
                                                     sem.at[0, slot]))
                    out.append(pltpu.make_async_copy(cv_hbm.at[layer, pid, :, h, :], vbuf.at[slot, h, j],
                                                     sem.at[1, slot]))
        return out

    @pl.when(b == 0)
    def _():
        for cp in copies(0, 0):
            cp.start()

    slot = b % 2

    @pl.when(b + 1 < n_seq)
    def _():
        for cp in copies(b + 1, 1 - slot):
            cp.start()

    for cp in copies(b, slot):
        cp.wait()

    for h in range(nh):
        q = q_ref[0, :, h * HD_ATT:(h + 1) * HD_ATT]
        kn = kn_ref[0, :, h * HD_ATT:(h + 1) * HD_ATT]
        vn = vn_ref[0, :, h * HD_ATT:(h + 1) * HD_ATT]
        kh = kbuf[slot, h].reshape(n_pg * page, HD_ATT)
        vh = vbuf[slot, h].reshape(n_pg * page, HD_ATT)
        s0 = jnp.sum(q * kn, axis=1, keepdims=True) * scale
        s = jnp.sum(kh * q, axis=1, keepdims=True) * scale
        m = jnp.maximum(jnp.max(s, axis=0, keepdims=True), s0)
        p = jnp.exp(s - m)
        p0 = jnp.exp(s0 - m)
        den = jnp.sum(p, axis=0, keepdims=True) + p0
        num = jnp.sum(p * vh, axis=0, keepdims=True) + p0 * vn
        o_ref[0, :, h * HD_ATT:(h + 1) * HD_ATT] = (num / den).astype(BF16)


def _moba_sample(q, k_new, v_new, cache_k, cache_v, layer, page_table_flat, sel_idx_flat, n_seq, n_pages, n_slots):
    _, _, page, nh, hd = cache_k.shape
    da = q.shape[1]
    assert MOBA_BLOCK == 2 * page and hd == HD_ATT
    kern = functools.partial(_moba_sample_kernel, layer=layer, scale=HD_ATT ** -0.5, n_slots=n_slots,
                             n_pages=n_pages, nh=nh, page=page, n_seq=n_seq)
    row = pl.BlockSpec((1, 1, da), lambda b, pt, idx: (b, 0, 0))
    out = pl.pallas_call(
        kern,
        out_shape=jax.ShapeDtypeStruct((n_seq, 1, da), BF16),
        grid_spec=pltpu.PrefetchScalarGridSpec(
            num_scalar_prefetch=2,
            grid=(n_seq,),
            in_specs=[row, row, row, pl.BlockSpec(memory_space=pl.ANY), pl.BlockSpec(memory_space=pl.ANY)],
            out_specs=row,
            scratch_shapes=[pltpu.VMEM((2, nh, 2 * n_slots, page, hd), F32),
                            pltpu.VMEM((2, nh, 2 * n_slots, page, hd), F32),
                            pltpu.SemaphoreType.DMA((2, 2))]),
        compiler_params=_cparams("arbitrary"),
        name="moba_sample",
    )(page_table_flat, sel_idx_flat, q.reshape(n_seq, 1, da), k_new.reshape(n_seq, 1, da),
      v_new.reshape(n_seq, 1, da), cache_k, cache_v)
    return out.reshape(n_seq, da)


def _shift_rows(x, tail_ref, first_ref, seq):
    tm = x.shape[0]
    local = lax.broadcasted_iota(jnp.int32, (tm, 1), 0)
    seq_off = lax.rem(pl.program_id(0) * tm, seq)
    prev = jnp.where(local == 0, tail_ref[7:8, :], pltpu.roll(x, 1, 0))
    return jnp.where((local + seq_off) == 0, first_ref[0], prev)


def _lora1_kernel(h_ref, hp_ref, *rest, seq):
    if seq is None:
        mu_ref, w1_ref, a1_ref, g1_ref, tw_ref, ta_ref, tg_ref = rest
        h = h_ref[...]
        hp = hp_ref[...]
    else:
        first_ref, mu_ref, w1_ref, a1_ref, g1_ref, tw_ref, ta_ref, tg_ref = rest
        h = h_ref[...]
        hp = _shift_rows(h, hp_ref, first_ref, seq)
    dx = hp - h
    mu = mu_ref[...]
    xw = (h + dx * mu[0:1, :]).astype(BF16)
    xa = (h + dx * mu[1:2, :]).astype(BF16)
    xg = (h + dx * mu[2:3, :]).astype(BF16)
    tw_ref[...] = jnp.tanh(_dot(xw, w1_ref[...])).astype(BF16)
    ta_ref[...] = _dot(xa, a1_ref[...]).astype(BF16)
    tg_ref[...] = jax.nn.sigmoid(_dot(xg, g1_ref[...])).astype(BF16)


def _prev_specs(tm, width, seq):
    tail = pl.BlockSpec((8, width), lambda i: (jnp.maximum(i * (tm // 8) - 1, 0), 0))
    first = pl.BlockSpec((1, 1, width), lambda i: (i * tm // seq, 0, 0))
    return tail, first


def _lora1(h, hprev, mu_wag, w1, a1, g1, seq=None):
    m, d = h.shape
    tm = _tile(m if seq is None else seq, 256, 16)
    nw, na, ng = w1.shape[1], a1.shape[1], g1.shape[1]
    row = lambda i: (i, 0)
    full = lambda i: (0, 0)
    if seq is None:
        prev_specs, prev_args = [pl.BlockSpec((tm, d), row)], [hprev]
    else:
        prev_specs, prev_args = list(_prev_specs(tm, d, seq)), [h, hprev]
    return pl.pallas_call(
        functools.partial(_lora1_kernel, seq=seq),
        out_shape=[jax.ShapeDtypeStruct((m, nw), BF16), jax.ShapeDtypeStruct((m, na), BF16),
                   jax.ShapeDtypeStruct((m, ng), BF16)],
        grid=(m // tm,),
        in_specs=[pl.BlockSpec((tm, d), row)] + prev_specs + [pl.BlockSpec(mu_wag.shape, full),
                  pl.BlockSpec((d, nw), full), pl.BlockSpec((d, na), full), pl.BlockSpec((d, ng), full)],
        out_specs=[pl.BlockSpec((tm, nw), row), pl.BlockSpec((tm, na), row), pl.BlockSpec((tm, ng), row)],
        compiler_params=_cparams("arbitrary"),
        name="rwkv_lora1",
    )(h, *prev_args, mu_wag, w1, a1, g1)


def _head_sums(x, bd):
    hi = x.astype(BF16)
    r1 = x - hi.astype(F32)
    mid = r1.astype(BF16)
    lo = (r1 - mid.astype(F32)).astype(BF16)
    return _dot(hi, bd) + _dot(mid, bd) + _dot(lo, bd)


def _block_diag_ones():
    a = lax.broadcasted_iota(jnp.int32, (LANES, LANES), 0) // HD_RWKV
    b = lax.broadcasted_iota(jnp.int32, (LANES, LANES), 1) // HD_RWKV
    return jnp.where(a == b, 1.0, 0.0).astype(BF16)


def _rwkv_prep_kernel(p_ref, pp_ref, *rest, seq):
    if seq is None:
        first_ref = None
        (tw_ref, ta_ref, tg_ref, w2_ref, a2_ref, g2_ref, mu_ref, vec_ref,
         r_ref, dec_ref, km_ref, v_ref, kk_ref, be_ref, g_ref, bo_ref) = rest
    else:
        (first_ref, tw_ref, ta_ref, tg_ref, w2_ref, a2_ref, g2_ref, mu_ref, vec_ref,
         r_ref, dec_ref, km_ref, v_ref, kk_ref, be_ref, g_ref, bo_ref) = rest
    dr = r_ref.shape[1]
    bd = _block_diag_ones()
    mu = mu_ref[...]
    vec = vec_ref[...]
    w0, a0, k_k, k_a, r_k = vec[0:1, :], vec[1:2, :], vec[2:3, :], vec[3:4, :], vec[4:5, :]
    p_all = p_ref[...]
    pp_all = pp_ref[...] if seq is None else _shift_rows(p_all, pp_ref, first_ref, seq)

    def shifted(c):
        pc = p_all[:, c * dr:(c + 1) * dr]
        return pc + (pp_all[:, c * dr:(c + 1) * dr] - pc) * mu[c:c + 1, :]

    r = shifted(0)
    k = shifted(1)
    v = shifted(2)
    w = -jnp.logaddexp(-(w0 + _dot(tw_ref[...], w2_ref[...])), 0.0) - 0.5
    dec = jnp.exp(-jnp.exp(w))
    a = jax.nn.sigmoid(a0 + _dot(ta_ref[...], a2_ref[...]))
    g = _dot(tg_ref[...], g2_ref[...])
    kk = k * k_k
    kmod = k * (1.0 + (a - 1.0) * k_a)
    rk = r * kmod * r_k
    for c in range(dr // LANES):
        sl = slice(c * LANES, (c + 1) * LANES)
        kkc = kk[:, sl]
        kkc = kkc * lax.rsqrt(jnp.maximum(_head_sums(kkc * kkc, bd), 1e-24))
        kk_ref[:, sl] = kkc
        be_ref[:, sl] = kkc * a[:, sl]
        bo_ref[:, sl] = _head_sums(rk[:, sl], bd) * v[:, sl]
    r_ref[...] = r
    dec_ref[...] = dec
    km_ref[...] = kmod
    v_ref[...] = v
    g_ref[...] = g


def _rwkv_prep(p, pprev, tw, ta, tg, w2, a2, g2, mu_rkv, vecs, seq=None):
    m = p.shape[0]
    dr = p.shape[1] // 3
    tm = _tile(m if seq is None else seq, 128, 16)
    row = lambda i: (i, 0)
    full = lambda i: (0, 0)
    outs = [jax.ShapeDtypeStruct((m, dr), F32)] * 8
    if seq is None:
        prev_specs, prev_args = [pl.BlockSpec((tm, 3 * dr), row)], [pprev]
    else:
        prev_specs, prev_args = list(_prev_specs(tm, 3 * dr, seq)), [p, pprev]
    return pl.pallas_call(
        functools.partial(_rwkv_prep_kernel, seq=seq),
        out_shape=outs,
        grid=(m // tm,),
        in_specs=[pl.BlockSpec((tm, 3 * dr), row)] + prev_specs + [
                  pl.BlockSpec((tm, tw.shape[1]), row), pl.BlockSpec((tm, ta.shape[1]), row),
                  pl.BlockSpec((tm, tg.shape[1]), row),
                  pl.BlockSpec(w2.shape, full), pl.BlockSpec(a2.shape, full), pl.BlockSpec(g2.shape, full),
                  pl.BlockSpec(mu_rkv.shape, full), pl.BlockSpec(vecs.shape, full)],
        out_specs=[pl.BlockSpec((tm, dr), row)] * 8,
        compiler_params=_cparams("arbitrary"),
        name="rwkv_prep",
    )(p, *prev_args, tw, ta, tg, w2, a2, g2, mu_rkv, vecs)


SCAN_STEPS = 8


def _segment_ones(width):
    a = lax.broadcasted_iota(jnp.int32, (width, width), 0) // HD_RWKV
    b = lax.broadcasted_iota(jnp.int32, (width, width), 1) // HD_RWKV
    return jnp.where(a == b, 1.0, 0.0).astype(BF16)


def _rwkv_scan_kernel(kk_ref, dec_ref, be_ref, km_ref, r_ref, v_ref, s0_ref, y_ref, s_ref, *, tc, npair):
    c = pl.program_id(1)
    n = HD_RWKV
    gs = min(SCAN_STEPS, tc)
    wide = 2 if npair % 2 == 0 else 1
    assert tc % gs == 0

    @pl.when(c == 0)
    def _():
        s_ref[...] = s0_ref[...]

    seg = _segment_ones(wide * LANES)
    lane = lax.broadcasted_iota(jnp.int32, (n, LANES), 1)
    sub = lax.broadcasted_iota(jnp.int32, (n, LANES), 0)
    diag = (lane % n) == sub

    def stack(parts):
        rows = [jnp.concatenate(parts[i:i + wide], axis=1) for i in range(0, npair, wide)]
        return jnp.concatenate(rows, axis=0).astype(BF16)

    def part(full, p):
        return full[(p // wide) * n:(p // wide + 1) * n, (p % wide) * LANES:(p % wide + 1) * LANES]

    def steps(t0):
        def row(ref, p, r):
            return ref[pl.ds(t0, gs), p * LANES:(p + 1) * LANES][r:r + 1, :]

        y_rows = [[None] * gs for _ in range(npair)]

        def finish(y_all, r):
            for p in range(npair):
                y_rows[p][r] = jnp.sum(jnp.where(diag, part(y_all, p), 0.0), axis=0, keepdims=True)

        pending = None
        for r in range(gs):
            sa_all = _dot(stack([s_ref[0, p] * row(kk_ref, p, r) for p in range(npair)]), seg)
            vb_all = _dot(stack([jnp.where(diag, row(v_ref, p, r), 0.0) for p in range(npair)]), seg)
            if pending is not None:
                finish(pending, r - 1)
            ys = []
            for p in range(npair):
                s_new = (s_ref[0, p] * row(dec_ref, p, r) - part(sa_all, p) * row(be_ref, p, r)
                         + part(vb_all, p) * row(km_ref, p, r))
                s_ref[0, p] = s_new
                ys.append(s_new * row(r_ref, p, r))
            pending = _dot(stack(ys), seg)
        finish(pending, gs - 1)
        for p in range(npair):
            y_ref[pl.ds(t0, gs), p * LANES:(p + 1) * LANES] = jnp.concatenate(y_rows[p], axis=0)

    if tc == gs:
        steps(0)
    else:
        def body(i, carry):
            steps(pl.multiple_of(i * gs, gs))
            return carry

        lax.fori_loop(0, tc // gs, body, 0)


def _rwkv_scan(kk, dec, be, km, r, v, s0, batch, seq):
    m, dr = kk.shape
    npair = dr // LANES
    tc = _tile(seq, 128, 8)
    nc = seq // tc
    if seq == 1:
        kk, dec, be, km, r, v = (a.reshape(m, 1, dr) for a in (kk, dec, be, km, r, v))
        op = pl.BlockSpec((None, 1, dr), lambda b, c: (b, 0, 0))
        y_shape = jax.ShapeDtypeStruct((m, 1, dr), F32)
    else:
        op = pl.BlockSpec((tc, dr), lambda b, c: (b * nc + c, 0))
        y_shape = jax.ShapeDtypeStruct((m, dr), F32)
    kern = functools.partial(_rwkv_scan_kernel, tc=tc, npair=npair)
    st = pl.BlockSpec((1, npair, HD_RWKV, LANES), lambda b, c: (b, 0, 0, 0))
    y, s_fin = pl.pallas_call(
        kern,
        out_shape=[y_shape, jax.ShapeDtypeStruct((batch, npair, HD_RWKV, LANES), F32)],
        grid=(batch, nc),
        in_specs=[op] * 6 + [st],
        out_specs=[op, st],
        compiler_params=_cparams("arbitrary", "arbitrary"),
        name="rwkv_scan",
    )(kk, dec, be, km, r, v, s0)
    return y.reshape(m, dr), s_fin


def _rwkv_post_kernel(y_ref, bo_ref, g_ref, ln_ref, o_ref):
    bd = _block_diag_ones()
    ln = ln_ref[...]
    inv_n = 1.0 / HD_RWKV
    for c in range(y_ref.shape[1] // LANES):
        sl = slice(c * LANES, (c + 1) * LANES)
        y = y_ref[:, sl]
        mean = _head_sums(y, bd) * inv_n
        d = y - mean
        var = _head_sums(d * d, bd) * inv_n
        yn = d * lax.rsqrt(var + GN_EPS) * ln[0:1, sl] + ln[1:2, sl]
        o_ref[:, sl] = ((yn + bo_ref[:, sl]) * g_ref[:, sl]).astype(BF16)


def _rwkv_post(y, bonus, g, ln):
    m, dr = y.shape
    tm = _tile(m, 256, 16)
    row = lambda i: (i, 0)
    return pl.pallas_call(
        _rwkv_post_kernel,
        out_shape=jax.ShapeDtypeStruct((m, dr), BF16),
        grid=(m // tm,),
        in_specs=[pl.BlockSpec((tm, dr), row)] * 3 + [pl.BlockSpec(ln.shape, lambda i: (0, 0))],
        out_specs=pl.BlockSpec((tm, dr), row),
        compiler_params=_cparams("arbitrary"),
        name="rwkv_post",
    )(y, bonus, g, ln)


def _pack_state(s):
    b, h, n, _ = s.shape
    return s.reshape(b, h // 2, 2, n, n).transpose(0, 1, 3, 2, 4).reshape(b, h // 2, n, 2 * n)


def _unpack_state(s):
    b, p, n, _ = s.shape
    return s.reshape(b, p, n, 2, n).transpose(0, 1, 3, 2, 4).reshape(b, 2 * p, n, n)


def _pad_cols(w, n):
    return jnp.pad(w, ((0, 0), (0, n - w.shape[1])))


def _pad_rows(w, n):
    return jnp.pad(w, ((0, n - w.shape[0]), (0, 0)))


def _rope_tables(pos, rows_per_pos):
    half = HD_ATT // 2
    inv = ROPE_THETA ** (-jnp.arange(half, dtype=F32) / half)
    ang = pos.astype(F32)[:, None] * inv[None, :]
    cos = jnp.cos(ang)
    sin = jnp.sin(ang)
    cos2 = jnp.concatenate([cos, cos], axis=-1)
    sin2 = jnp.concatenate([-sin, sin], axis=-1)
    if rows_per_pos is not None:
        cos2 = jnp.broadcast_to(cos2, (rows_per_pos, HD_ATT))
        sin2 = jnp.broadcast_to(sin2, (rows_per_pos, HD_ATT))
    return cos2, sin2


def kernel(x_prompt, x_sample, cache_k, cache_v, state_shift, state_wkv, page_table, c_prompt, c_sample, g_norm, w_ada, b_ada, ffn_w_gate, ffn_w_up, ffn_w_down, w_in, w_out, mu_rkv, mu_wag, w0, w_lora1, w_lora2, a0, a_lora1, a_lora2, g_lora1, g_lora2, k_k, k_a, r_k, ln_x_w, ln_x_b, g_final):
    B, T, D = x_prompt.shape
    DB, DS, _ = x_sample.shape
    depth = g_norm.shape[0]
    assert DS == 1
    n_pages = page_table.shape[1]
    page = cache_k.shape[2]
    past_len = n_pages * page
    assert past_len % MOBA_BLOCK == 0 and MOBA_BLOCK == 2 * page
    da = cache_k.shape[3] * cache_k.shape[4]
    nh = da // HD_ATT
    dr = D - da
    n_slots = min(MOBA_TOPK, past_len // MOBA_BLOCK)

    xp = x_prompt.reshape(B * T, D)
    xs = x_sample.reshape(DB, D)
    pt_flat = page_table.reshape(-1).astype(jnp.int32)
    cos_p, sin_p = _rope_tables(jnp.arange(T, dtype=jnp.int32), None)
    cos_p = jnp.tile(cos_p, (B, 1))
    sin_p = jnp.tile(sin_p, (B, 1))
    cos_s, sin_s = _rope_tables(jnp.full((1,), past_len, dtype=jnp.int32), DB)

    outs = [[] for _ in range(8)]
    n_c = B + DB
    n_cp = -(-n_c // 8) * 8
    c_all = jnp.pad(jnp.concatenate([c_prompt, c_sample], axis=0), ((0, n_cp - n_c), (0, 0)))

    for l in range(depth):
        bf = lambda w: w.astype(BF16)
        wg, wu, wd = ffn_w_gate[l], ffn_w_up[l], bf(ffn_w_down[l])
        win, wout = w_in[l], bf(w_out[l])
        lw = -(-w_lora1.shape[2] // LANES) * LANES
        la = -(-a_lora1.shape[2] // LANES) * LANES
        lg = -(-g_lora1.shape[2] // LANES) * LANES
        w1, a1, g1 = bf(_pad_cols(w_lora1[l], lw)), bf(_pad_cols(a_lora1[l], la)), bf(_pad_cols(g_lora1[l], lg))
        w2, a2, g2 = bf(_pad_rows(w_lora2[l], lw)), bf(_pad_rows(a_lora2[l], la)), bf(_pad_rows(g_lora2[l], lg))
        vecs = jnp.stack([w0[l], a0[l], k_k[l], k_a[l], r_k[l].reshape(dr)], axis=0)
        vecs = jnp.pad(vecs, ((0, 3), (0, 0)))
        ln = jnp.pad(jnp.stack([ln_x_w[l], ln_x_b[l]], axis=0), ((0, 6), (0, 0)))
        mu3 = jnp.pad(mu_rkv[l], ((0, 5), (0, 0)))
        muw = jnp.pad(mu_wag[l], ((0, 5), (0, 0)))

        mod = _ada(c_all, w_ada[l], b_ada[l])
        chunks = [mod[:, k * D:(k + 1) * D] for k in range(9)]
        mod_p = [ch[:B].reshape(B, 1, D) for ch in chunks]
        mod_s = [ch[B:B + DB].reshape(1, DB, D) for ch in chunks]

        def ffn(x, mods, rpg, idx):
            sub = 2 * idx
            sh, sc, ga = mods[3 * sub], mods[3 * sub + 1], mods[3 * sub + 2]
            (h,) = _modulate(x, g_norm[l, sub], sh, sc, rpg, False)
            u = _ffn_up(h, wg[idx], wu[idx])
            return _ffn_down(u, wd[idx], x, ga, rpg)

        xp = ffn(xp, mod_p, T, 0)
        hb, hf = _modulate(xp, g_norm[l, 1], mod_p[3], mod_p[4], T, True)
        q = _proj(hb, win, 0, da, (cos_p, sin_p))
        k = _proj(hb, win, da, da, (cos_p, sin_p))
        v = _proj(hb, win, 2 * da, da)
        prk = _proj(hb, win, 3 * da, 3 * dr)
        att = _moba_prompt(q, k, v, B, T)

        xs = ffn(xs, mod_s, 1, 0)
        hbs, hfs = _modulate(xs, g_norm[l, 1], mod_s[3], mod_s[4], 1, True)
        qs = _proj(hbs, win, 0, da, (cos_s, sin_s))
        ks = _proj(hbs, win, da, da, (cos_s, sin_s))
        vs = _proj(hbs, win, 2 * da, da)
        n_prev = DB + B
        n_prev_p = -(-n_prev // 16) * 16
        prev_rows = jnp.concatenate([state_shift[l], jnp.zeros((B, D), F32),
                                     jnp.zeros((n_prev_p - n_prev, D), F32)], axis=0)
        p_first = _proj(bf(prev_rows), win, 3 * da, 3 * dr)
        prk_s = _proj(hbs, win, 3 * da, 3 * dr)

        h3 = hf.reshape(B, T, D)

        def rwkv(hf_, hprev_, prk_, pprev_, s0, nb_, nt_):
            seq = None if nt_ == 1 else nt_
            tw, ta, tg = _lora1(hf_, hprev_, muw, w1, a1, g1, seq)
            r_, dec_, km_, v_, kk_, be_, g_, bo_ = _rwkv_prep(prk_, pprev_, tw, ta, tg, w2, a2, g2, mu3, vecs, seq)
            y, s_fin = _rwkv_scan(kk_, dec_, be_, km_, r_, v_, s0, nb_, nt_)
            rw = _rwkv_post(y, bo_, g_, ln)
            return rw, s_fin

        s0_p = jnp.zeros((B, dr // LANES, HD_RWKV, LANES), F32)
        rw_p, sfin_p = rwkv(hf, jnp.zeros((B, 1, D), F32), prk, p_first[DB:DB + B].reshape(B, 1, 3 * dr),
                            s0_p, B, T)
        xp = _out_proj(att, rw_p, wout, xp, mod_p[5], T)
        xp = ffn(xp, mod_p, T, 1)

        n_blocks = past_len // MOBA_BLOCK
        kmeans = _block_means(cache_k, l, pt_flat)
        sel = _select_blocks(kmeans, qs, DB, n_blocks, nh, n_slots)
        att_s = _moba_sample(qs, ks, vs, cache_k, cache_v, l, pt_flat, sel.reshape(-1), DB, n_pages, n_slots)

        rw_s, sfin_s = rwkv(hfs, state_shift[l], prk_s, p_first[:DB], _pack_state(state_wkv[l]), DB, 1)
        xs = _out_proj(att_s, rw_s, wout, xs, mod_s[5], 1)
        xs = ffn(xs, mod_s, 1, 1)

        outs[0].append(k.reshape(B, T, nh, HD_ATT))
        outs[1].append(v.reshape(B, T, nh, HD_ATT))
        outs[2].append(h3[:, -1])
        outs[3].append(_unpack_state(sfin_p))
        outs[4].append(ks.reshape(DB, 1, nh, HD_ATT))
        outs[5].append(vs.reshape(DB, 1, nh, HD_ATT))
        outs[6].append(hfs)
        outs[7].append(_unpack_state(sfin_s))

    y_prompt = _rmsnorm(xp, g_final).reshape(B, T, D)
    y_sample = _rmsnorm(xs, g_final).reshape(DB, 1, D)
    return (y_prompt, y_sample) + tuple(jnp.stack(o) for o in outs)
```

```python
import functools

import jax
import jax.numpy as jnp
from jax import lax
from jax.experimental import pallas as pl
from jax.experimental.pallas import tpu as pltpu

F32 = jnp.float32
BF16 = jnp.bfloat16

HD_ATT = 128
HD_RWKV = 64
LANES = 128
MOBA_BLOCK = 256
MOBA_TOPK = 3
ROPE_THETA = 10000.0
RMS_EPS = 1e-6
GN_EPS = 64e-5
NEG_BIG = -1e30
VMEM_LIMIT = 56 * 1024 * 1024


def _cparams(*sem):
    return pltpu.CompilerParams(dimension_semantics=sem, vmem_limit_bytes=VMEM_LIMIT)


def _tile(n, pref, align):
    if n <= pref:
        return n
    t = (pref // align) * align
    while t >= align:
        if n % t == 0:
            return t
        t -= align
    return n


def _dot(a, b):
    return jnp.dot(a, b, preferred_element_type=F32)


def _dot_nt(a, b):
    return lax.dot_general(a, b, (((1,), (1,)), ((), ())), preferred_element_type=F32)


def _ada_kernel(c_ref, w_ref, b_ref, o_ref):
    c = c_ref[...]
    s = (c * jax.nn.sigmoid(c)).astype(BF16)
    o_ref[...] = _dot(s, w_ref[...].astype(BF16)) + b_ref[...]


def _ada(c, w, b):
    m, d = c.shape
    n = w.shape[1]
    tn = _tile(n, 512, LANES)
    return pl.pallas_call(
        _ada_kernel,
        out_shape=jax.ShapeDtypeStruct((m, n), F32),
        grid=(n // tn,),
        in_specs=[pl.BlockSpec((m, d), lambda j: (0, 0)),
                  pl.BlockSpec((d, tn), lambda j: (0, j)),
                  pl.BlockSpec((1, tn), lambda j: (0, j))],
        out_specs=pl.BlockSpec((m, tn), lambda j: (0, j)),
        compiler_params=_cparams("arbitrary"),
        name="ada_proj",
    )(c, w, b.reshape(1, n))


def _modulate_kernel(x_ref, g_ref, sh_ref, sc_ref, *o_refs):
    x = x_ref[...]
    y = x * lax.rsqrt(jnp.mean(x * x, axis=-1, keepdims=True) + RMS_EPS) * g_ref[...]
    h = y * (1.0 + sc_ref[0]) + sh_ref[0]
    o_refs[0][...] = h.astype(BF16)
    if len(o_refs) > 1:
        o_refs[1][...] = h


def _modulate(x, g, sh, sc, rows_per_group, want_f32):
    m, d = x.shape
    r = sh.shape[1]
    tm = m if r > 1 else _tile(rows_per_group, 256, 16)
    grp = (lambda i: (0, 0, 0)) if r > 1 else (lambda i: (i * tm // rows_per_group, 0, 0))
    out_shape = [jax.ShapeDtypeStruct((m, d), BF16)]
    out_specs = [pl.BlockSpec((tm, d), lambda i: (i, 0))]
    if want_f32:
        out_shape.append(jax.ShapeDtypeStruct((m, d), F32))
        out_specs.append(pl.BlockSpec((tm, d), lambda i: (i, 0)))
    return pl.pallas_call(
        _modulate_kernel,
        out_shape=out_shape,
        grid=(m // tm,),
        in_specs=[pl.BlockSpec((tm, d), lambda i: (i, 0)),
                  pl.BlockSpec((1, d), lambda i: (0, 0)),
                  pl.BlockSpec((1, r, d), grp),
                  pl.BlockSpec((1, r, d), grp)],
        out_specs=out_specs,
        compiler_params=_cparams("arbitrary"),
        name="modulate",
    )(x, g.reshape(1, d), sh, sc)


def _rmsnorm_kernel(x_ref, g_ref, o_ref):
    x = x_ref[...]
    o_ref[...] = x * lax.rsqrt(jnp.mean(x * x, axis=-1, keepdims=True) + RMS_EPS) * g_ref[...]


def _rmsnorm(x, g):
    m, d = x.shape
    tm = _tile(m, 256, 8)
    return pl.pallas_call(
        _rmsnorm_kernel,
        out_shape=jax.ShapeDtypeStruct((m, d), F32),
        grid=(m // tm,),
        in_specs=[pl.BlockSpec((tm, d), lambda i: (i, 0)), pl.BlockSpec((1, d), lambda i: (0, 0))],
        out_specs=pl.BlockSpec((tm, d), lambda i: (i, 0)),
        compiler_params=_cparams("arbitrary"),
        name="final_rmsnorm",
    )(x, g.reshape(1, d))


def _ffn_up_kernel(h_ref, wg_ref, wu_ref, *rest):
    h = h_ref[...]
    g = _dot(h, wg_ref[...].astype(BF16))
    u = _dot(h, wu_ref[...].astype(BF16))
    if len(rest) == 1:
        (o_ref,) = rest
    else:
        wd_ref, o_ref, wdb_ref = rest

        @pl.when(pl.program_id(0) == 0)
        def _():
            wdb_ref[...] = wd_ref[...].astype(BF16)

    o_ref[...] = (g * jax.nn.sigmoid(g) * u).astype(BF16)


def _ffn_up(h, wg_all, wu_all, layer, idx, wd_all=None):
    m, d = h.shape
    f = wg_all.shape[-1]
    tm = _tile(m, 1024, 16)
    tn = _tile(f, 256, LANES)
    nj = f // tn
    w_spec = pl.BlockSpec((None, None, d, tn), lambda i, j: (layer, idx, 0, j))
    in_specs = [pl.BlockSpec((tm, d), lambda i, j: (i, 0)), w_spec, w_spec]
    out_shape = [jax.ShapeDtypeStruct((m, f), BF16)]
    out_specs = [pl.BlockSpec((tm, tn), lambda i, j: (i, j))]
    args = [h, wg_all, wu_all]
    if wd_all is not None:
        slab = lambda i, j: jnp.where(i == 0, j, nj - 1)
        in_specs.append(pl.BlockSpec((None, None, tn, d), lambda i, j: (layer, idx, slab(i, j), 0)))
        out_shape.append(jax.ShapeDtypeStruct((f, d), BF16))
        out_specs.append(pl.BlockSpec((tn, d), lambda i, j: (slab(i, j), 0)))
        args.append(wd_all)
    res = pl.pallas_call(
        _ffn_up_kernel,
        out_shape=out_shape,
        grid=(m // tm, nj),
        in_specs=in_specs,
        out_specs=out_specs,
        compiler_params=_cparams("arbitrary", "arbitrary"),
        name="ffn_up",
    )(*args)
    return res if wd_all is not None else res[0]


def _ffn_down_kernel(u_ref, w_ref, x_ref, ga_ref, o_ref):
    o_ref[...] = x_ref[...] + 0.5 * ga_ref[0] * _dot(u_ref[...], w_ref[...])


def _ffn_down(u, wd, x, ga, rows_per_group):
    m, f = u.shape
    d = wd.shape[1]
    r = ga.shape[1]
    tm = m if r > 1 else _tile(rows_per_group, 512, 16)
    tn = _tile(d, 256, LANES)
    grp = (lambda i, j: (0, 0, j)) if r > 1 else (lambda i, j: (i * tm // rows_per_group, 0, j))
    return pl.pallas_call(
        _ffn_down_kernel,
        out_shape=jax.ShapeDtypeStruct((m, d), F32),
        grid=(m // tm, d // tn),
        in_specs=[pl.BlockSpec((tm, f), lambda i, j: (i, 0)),
                  pl.BlockSpec((f, tn), lambda i, j: (0, j)),
                  pl.BlockSpec((tm, tn), lambda i, j: (i, j)),
                  pl.BlockSpec((1, r, tn), grp)],
        out_specs=pl.BlockSpec((tm, tn), lambda i, j: (i, j)),
        compiler_params=_cparams("arbitrary", "arbitrary"),
        name="ffn_down",
    )(u, wd, x, ga)


def _proj_kernel(h_ref, w_ref, o_ref):
    o_ref[...] = _dot(h_ref[...], w_ref[...].astype(BF16))


def _proj_rope_kernel(h_ref, w_ref, cos_ref, sin_ref, o_ref):
    x = _dot(h_ref[...], w_ref[...].astype(BF16))
    cos = cos_ref[...]
    sin = sin_ref[...]
    for c in range(x.shape[1] // HD_ATT):
        xc = x[:, c * HD_ATT:(c + 1) * HD_ATT]
        o_ref[:, c * HD_ATT:(c + 1) * HD_ATT] = xc * cos + pltpu.roll(xc, HD_ATT // 2, 1) * sin


def _proj(h, w, layer, col0, ncols, rope=None):
    m, d = h.shape
    tm = _tile(m, 1024, 16)
    tn = _tile(ncols, 256, LANES)
    j0 = col0 // tn
    in_specs = [pl.BlockSpec((tm, d), lambda i, j: (i, 0)),
                pl.BlockSpec((None, d, tn), lambda i, j: (layer, 0, j + j0))]
    args = [h, w]
    kern = _proj_kernel
    if rope is not None:
        in_specs += [pl.BlockSpec((tm, HD_ATT), lambda i, j: (i, 0)),
                     pl.BlockSpec((tm, HD_ATT), lambda i, j: (i, 0))]
        args += list(rope)
        kern = _proj_rope_kernel
    return pl.pallas_call(
        kern,
        out_shape=jax.ShapeDtypeStruct((m, ncols), F32),
        grid=(m // tm, ncols // tn),
        in_specs=in_specs,
        out_specs=pl.BlockSpec((tm, tn), lambda i, j: (i, j)),
        compiler_params=_cparams("arbitrary", "arbitrary"),
        name="in_proj_rope" if rope is not None else "in_proj",
    )(*args)


def _out_proj_kernel(att_ref, rw_ref, wa_ref, wr_ref, x_ref, ga_ref, o_ref):
    y = _dot(att_ref[...], wa_ref[...]) + _dot(rw_ref[...], wr_ref[...])
    o_ref[...] = x_ref[...] + ga_ref[0] * y


def _out_proj(att, rw, w, x, ga, rows_per_group):
    m, da = att.shape
    dr = rw.shape[1]
    d = w.shape[1]
    assert da == dr
    r = ga.shape[1]
    tm = m if r > 1 else _tile(rows_per_group, 1024, 16)
    tn = _tile(d, 256, LANES)
    grp = (lambda i, j: (0, 0, j)) if r > 1 else (lambda i, j: (i * tm // rows_per_group, 0, j))
    return pl.pallas_call(
        _out_proj_kernel,
        out_shape=jax.ShapeDtypeStruct((m, d), F32),
        grid=(m // tm, d // tn),
        in_specs=[pl.BlockSpec((tm, da), lambda i, j: (i, 0)),
                  pl.BlockSpec((tm, dr), lambda i, j: (i, 0)),
                  pl.BlockSpec((da, tn), lambda i, j: (0, j)),
                  pl.BlockSpec((dr, tn), lambda i, j: (1, j)),
                  pl.BlockSpec((tm, tn), lambda i, j: (i, j)),
                  pl.BlockSpec((1, r, tn), grp)],
        out_specs=pl.BlockSpec((tm, tn), lambda i, j: (i, j)),
        compiler_params=_cparams("arbitrary", "arbitrary"),
        name="out_proj",
    )(att, rw, w, w, x, ga)


def _rank_select(g_rows, n_valid, topk):
    nb = len(g_rows)
    sel = []
    for j in range(nb):
        rank = jnp.zeros_like(g_rows[j])
        for n in range(nb):
            if n == j:
                continue
            beats = (g_rows[n] > g_rows[j]) if n > j else (g_rows[n] >= g_rows[j])
            rank = rank + jnp.where(beats, jnp.where(n < n_valid, 1.0, 0.0), 0.0)
        sel.append(rank < float(topk))
    return sel


MOBA_HEADS_PER_STEP = 4


def _moba_prompt_kernel(q_ref, k_ref, v_ref, o_ref, kb_ref, vt_ref, km_ref, m_ref, l_ref, acc_ref,
                        *, nb, scale, hp):
    i = pl.program_id(2)
    blk = MOBA_BLOCK
    heads = [slice(hh * HD_ATT, (hh + 1) * HD_ATT) for hh in range(hp)]

    @pl.when(i == 0)
    def _():
        km_ref[...] = jnp.zeros_like(km_ref)
        for hh in range(hp):
            for j in range(nb):
                kj = k_ref[j * blk:(j + 1) * blk, heads[hh]]
                km_ref[hh, j:j + 1, :] = jnp.mean(kj, axis=0, keepdims=True)
                kb_ref[hh, j] = kj.astype(BF16)
                vt_ref[hh, j] = jnp.transpose(v_ref[j * blk:(j + 1) * blk, heads[hh]]).astype(BF16)

    kpos = lax.broadcasted_iota(jnp.int32, (blk, blk), 0)
    qpos = lax.broadcasted_iota(jnp.int32, (blk, blk), 1)
    qb, sel = [], []
    for hh in range(hp):
        q = q_ref[:, heads[hh]]
        gt = _dot_nt(km_ref[hh].astype(BF16), q.astype(BF16))
        sel.append(_rank_select([gt[j:j + 1, :] for j in range(nb)], i, MOBA_TOPK))
        qb.append((q * scale).astype(BF16))

    for hh in range(hp):
        s = jnp.where(kpos <= qpos, _dot_nt(kb_ref[hh, i], qb[hh]), NEG_BIG)
        m0 = jnp.max(s, axis=0, keepdims=True)
        p = jnp.exp(s - m0)
        m_ref[hh] = m0
        l_ref[hh] = jnp.sum(p, axis=0, keepdims=True)
        acc_ref[hh] = _dot(vt_ref[hh, i], p.astype(BF16))

    for j in range(nb - 1):
        @pl.when(j < i)
        def _(j=j):
            for hh in range(hp):
                s = jnp.where(sel[hh][j], _dot_nt(kb_ref[hh, j], qb[hh]), NEG_BIG)
                m_old = m_ref[hh]
                m_new = jnp.maximum(m_old, jnp.max(s, axis=0, keepdims=True))
                alpha = jnp.exp(m_old - m_new)
                p = jnp.exp(s - m_new)
                m_ref[hh] = m_new
                l_ref[hh] = alpha * l_ref[hh] + jnp.sum(p, axis=0, keepdims=True)
                acc_ref[hh] = alpha * acc_ref[hh] + _dot(vt_ref[hh, j], p.astype(BF16))

    for hh in range(hp):
        o_ref[:, heads[hh]] = jnp.transpose(acc_ref[hh] / l_ref[hh]).astype(BF16)


def _moba_prompt(q, k, v, batch, seq):
    m, da = q.shape
    nh = da // HD_ATT
    assert seq % MOBA_BLOCK == 0
    nb = seq // MOBA_BLOCK
    nbp = -(-nb // 8) * 8
    hp = _tile(nh, MOBA_HEADS_PER_STEP, 1)
    w = hp * HD_ATT
    kern = functools.partial(_moba_prompt_kernel, nb=nb, scale=HD_ATT ** -0.5, hp=hp)
    return pl.pallas_call(
        kern,
        out_shape=jax.ShapeDtypeStruct((m, da), BF16),
        grid=(batch, nh // hp, nb),
        in_specs=[pl.BlockSpec((MOBA_BLOCK, w), lambda b, h, i: (b * nb + i, h)),
                  pl.BlockSpec((seq, w), lambda b, h, i: (b, h)),
                  pl.BlockSpec((seq, w), lambda b, h, i: (b, h))],
        out_specs=pl.BlockSpec((MOBA_BLOCK, w), lambda b, h, i: (b * nb + i, h)),
        scratch_shapes=[pltpu.VMEM((hp, nb, MOBA_BLOCK, HD_ATT), BF16),
                        pltpu.VMEM((hp, nb, HD_ATT, MOBA_BLOCK), BF16),
                        pltpu.VMEM((hp, nbp, HD_ATT), F32),
                        pltpu.VMEM((hp, 1, MOBA_BLOCK), F32),
                        pltpu.VMEM((hp, 1, MOBA_BLOCK), F32),
                        pltpu.VMEM((hp, HD_ATT, MOBA_BLOCK), F32)],
        compiler_params=_cparams("arbitrary", "arbitrary", "arbitrary"),
        name="moba_prompt",
    )(q, k, v)


MEAN_BLOCKS_PER_STEP = 4


def _block_mean_kernel(pt_ref, *refs):
    o_ref = refs[-1]
    for n in range(len(refs) // 2):
        tot = jnp.sum(refs[2 * n][...], axis=0) + jnp.sum(refs[2 * n + 1][...], axis=0)
        o_ref[n] = tot * (1.0 / MOBA_BLOCK)


def _block_means(cache_k, layer, page_table_flat):
    _, _, page, nh, hd = cache_k.shape
    n = page_table_flat.shape[0] // 2
    per = _tile(n, MEAN_BLOCKS_PER_STEP, 1)

    def page_map(e):
        return lambda s, pt: (layer, pt[2 * per * s + e], 0, 0, 0)

    return pl.pallas_call(
        _block_mean_kernel,
        out_shape=jax.ShapeDtypeStruct((n, nh, hd), F32),
        grid_spec=pltpu.PrefetchScalarGridSpec(
            num_scalar_prefetch=1,
            grid=(n // per,),
            in_specs=[pl.BlockSpec((None, None, page, nh, hd), page_map(e)) for e in range(2 * per)],
            out_specs=pl.BlockSpec((per, nh, hd), lambda s, pt: (s, 0, 0))),
        compiler_params=_cparams("arbitrary"),
        name="block_key_means",
    )(page_table_flat, *([cache_k] * (2 * per)))


def _select_kernel(km_ref, q_ref, o_ref, *, n_blocks, topk):
    q = q_ref[0]
    g = [jnp.sum(km_ref[0, n] * q, axis=1, keepdims=True) for n in range(n_blocks)]
    for s in range(topk):
        mx = g[0]
        for n in range(1, n_blocks):
            mx = jnp.maximum(mx, g[n])
        pick = jnp.full(mx.shape, n_blocks, jnp.int32)
        for n in range(n_blocks - 1, -1, -1):
            pick = jnp.where(g[n] == mx, n, pick)
        o_ref[0, s] = pick
        g = [jnp.where(pick == n, -jnp.inf, g[n]) for n in range(n_blocks)]


def _select_blocks(kmeans, q, n_seq, n_blocks, nh, topk):
    hd = HD_ATT
    kern = functools.partial(_select_kernel, n_blocks=n_blocks, topk=topk)
    return pl.pallas_call(
        kern,
        out_shape=jax.ShapeDtypeStruct((n_seq, topk, nh, 1), jnp.int32),
        grid=(n_seq,),
        in_specs=[pl.BlockSpec((1, n_blocks, nh, hd), lambda b: (b, 0, 0, 0)),
                  pl.BlockSpec((1, nh, hd), lambda b: (b, 0, 0))],
        out_specs=pl.BlockSpec((1, topk, nh, 1), lambda b: (b, 0, 0, 0)),
        compiler_params=_cparams("arbitrary"),
        name="moba_select",
    )(kmeans.reshape(n_seq, n_blocks, nh, hd), q.reshape(n_seq, nh, hd))


def _moba_sample_kernel(pt_ref, idx_ref, q_ref, kn_ref, vn_ref, ck_hbm, cv_hbm, o_ref, kbuf, vbuf, sem,
                        *, layer, scale, n_slots, n_pages, nh, page, n_seq):
    b = pl.program_id(0)
    n_pg = 2 * n_slots

    def copies(seq, slot):
        out = []
        for h in range(nh):
            for s in range(n_slots):
                blk = idx_ref[(seq * n_slots + s) * nh + h]
                for half in range(2):
                    pid = pt_ref[seq * n_pages + 2 * blk + half]
                    j = s * 2 + half
                    out.append(pltpu.make_async_copy(ck_hbm.at[layer, pid, :, h, :], kbuf.at[slot, h, j],
                                                     sem.at[0, slot]))
                    out.append(pltpu.make_async_copy(cv_hbm.at[layer, pid, :, h, :], vbuf.at[slot, h, j],
                                                     sem.at[1, slot]))
        return out

    @pl.when(b == 0)
    def _():
        for cp in copies(0, 0):
            cp.start()

    slot = b % 2

    @pl.when(b + 1 < n_seq)
    def _():
        for cp in copies(b + 1, 1 - slot):
            cp.start()

    for cp in copies(b, slot):
        cp.wait()

    for h in range(nh):
        q = q_ref[0, :, h * HD_ATT:(h + 1) * HD_ATT]
        kn = kn_ref[0, :, h * HD_ATT:(h + 1) * HD_ATT]
        vn = vn_ref[0, :, h * HD_ATT:(h + 1) * HD_ATT]
        kh = kbuf[slot, h].reshape(n_pg * page, HD_ATT)
        vh = vbuf[slot, h].reshape(n_pg * page, HD_ATT)
        s0 = jnp.sum(q * kn, axis=1, keepdims=True) * scale
        s = jnp.sum(kh * q, axis=1, keepdims=True) * scale
        m = jnp.maximum(jnp.max(s, axis=0, keepdims=True), s0)
        p = jnp.exp(s - m)
        p0 = jnp.exp(s0 - m)
        den = jnp.sum(p, axis=0, keepdims=True) + p0
        num = jnp.sum(p * vh, axis=0, keepdims=True) + p0 * vn
        o_ref[0, :, h * HD_ATT:(h + 1) * HD_ATT] = (num / den).astype(BF16)


def _moba_sample(q, k_new, v_new, cache_k, cache_v, layer, page_table_flat, sel_idx_flat, n_seq, n_pages, n_slots):
    _, _, page, nh, hd = cache_k.shape
    da = q.shape[1]
    assert MOBA_BLOCK == 2 * page and hd == HD_ATT
    kern = functools.partial(_moba_sample_kernel, layer=layer, scale=HD_ATT ** -0.5, n_slots=n_slots,
                             n_pages=n_pages, nh=nh, page=page, n_seq=n_seq)
    row = pl.BlockSpec((1, 1, da), lambda b, pt, idx: (b, 0, 0))
    out = pl.pallas_call(
        kern,
        out_shape=jax.ShapeDtypeStruct((n_seq, 1, da), BF16),
        grid_spec=pltpu.PrefetchScalarGridSpec(
            num_scalar_prefetch=2,
            grid=(n_seq,),
            in_specs=[row, row, row, pl.BlockSpec(memory_space=pl.ANY), pl.BlockSpec(memory_space=pl.ANY)],
            out_specs=row,
            scratch_shapes=[pltpu.VMEM((2, nh, 2 * n_slots, page, hd), F32),
                            pltpu.VMEM((2, nh, 2 * n_slots, page, hd), F32),
                            pltpu.SemaphoreType.DMA((2, 2))]),
        compiler_params=_cparams("arbitrary"),
        name="moba_sample",
    )(page_table_flat, sel_idx_flat, q.reshape(n_seq, 1, da), k_new.reshape(n_seq, 1, da),
      v_new.reshape(n_seq, 1, da), cache_k, cache_v)
    return out.reshape(n_seq, da)


def _shift_rows(x, tail_ref, first_ref, seq):
    tm = x.shape[0]
    local = lax.broadcasted_iota(jnp.int32, (tm, 1), 0)
    seq_off = lax.rem(pl.program_id(0) * tm, seq)
    prev = jnp.where(local == 0, tail_ref[7:8, :], pltpu.roll(x, 1, 0))
    return jnp.where((local + seq_off) == 0, first_ref[0], prev)


def _lora1_kernel(h_ref, hp_ref, *rest, seq):
    if seq is None:
        mu_ref, w1_ref, a1_ref, g1_ref, tw_ref, ta_ref, tg_ref = rest
        h = h_ref[...]
        hp = hp_ref[...]
    else:
        first_ref, mu_ref, w1_ref, a1_ref, g1_ref, tw_ref, ta_ref, tg_ref = rest
        h = h_ref[...]
        hp = _shift_rows(h, hp_ref, first_ref, seq)
    dx = hp - h
    mu = mu_ref[...]
    xw = (h + dx * mu[0:1, :]).astype(BF16)
    xa = (h + dx * mu[1:2, :]).astype(BF16)
    xg = (h + dx * mu[2:3, :]).astype(BF16)
    tw_ref[...] = jnp.tanh(_dot(xw, w1_ref[...])).astype(BF16)
    ta_ref[...] = _dot(xa, a1_ref[...]).astype(BF16)
    tg_ref[...] = jax.nn.sigmoid(_dot(xg, g1_ref[...])).astype(BF16)


def _prev_specs(tm, width, seq):
    tail = pl.BlockSpec((8, width), lambda i: (jnp.maximum(i * (tm // 8) - 1, 0), 0))
    first = pl.BlockSpec((1, 1, width), lambda i: (i * tm // seq, 0, 0))
    return tail, first


def _lora1(h, hprev, mu_wag, w1, a1, g1, seq=None):
    m, d = h.shape
    tm = _tile(m if seq is None else seq, 256, 16)
    nw, na, ng = w1.shape[1], a1.shape[1], g1.shape[1]
    row = lambda i: (i, 0)
    full = lambda i: (0, 0)
    if seq is None:
        prev_specs, prev_args = [pl.BlockSpec((tm, d), row)], [hprev]
    else:
        prev_specs, prev_args = list(_prev_specs(tm, d, seq)), [h, hprev]
    return pl.pallas_call(
        functools.partial(_lora1_kernel, seq=seq),
        out_shape=[jax.ShapeDtypeStruct((m, nw), BF16), jax.ShapeDtypeStruct((m, na), BF16),
                   jax.ShapeDtypeStruct((m, ng), BF16)],
        grid=(m // tm,),
        in_specs=[pl.BlockSpec((tm, d), row)] + prev_specs + [pl.BlockSpec(mu_wag.shape, full),
                  pl.BlockSpec((d, nw), full), pl.BlockSpec((d, na), full), pl.BlockSpec((d, ng), full)],
        out_specs=[pl.BlockSpec((tm, nw), row), pl.BlockSpec((tm, na), row), pl.BlockSpec((tm, ng), row)],
        compiler_params=_cparams("arbitrary"),
        name="rwkv_lora1",
    )(h, *prev_args, mu_wag, w1, a1, g1)


def _head_sums(x, bd):
    hi = x.astype(BF16)
    r1 = x - hi.astype(F32)
    mid = r1.astype(BF16)
    lo = (r1 - mid.astype(F32)).astype(BF16)
    return _dot(hi, bd) + _dot(mid, bd) + _dot(lo, bd)


def _block_diag_ones():
    a = lax.broadcasted_iota(jnp.int32, (LANES, LANES), 0) // HD_RWKV
    b = lax.broadcasted_iota(jnp.int32, (LANES, LANES), 1) // HD_RWKV
    return jnp.where(a == b, 1.0, 0.0).astype(BF16)


def _rwkv_prep_kernel(p_ref, pp_ref, *rest, seq):
    if seq is None:
        first_ref = None
        (tw_ref, ta_ref, tg_ref, w2_ref, a2_ref, g2_ref, mu_ref, vec_ref,
         r_ref, dec_ref, km_ref, v_ref, kk_ref, be_ref, g_ref, bo_ref) = rest
    else:
        (first_ref, tw_ref, ta_ref, tg_ref, w2_ref, a2_ref, g2_ref, mu_ref, vec_ref,
         r_ref, dec_ref, km_ref, v_ref, kk_ref, be_ref, g_ref, bo_ref) = rest
    dr = r_ref.shape[1]
    bd = _block_diag_ones()
    mu = mu_ref[...]
    vec = vec_ref[...]
    w0, a0, k_k, k_a, r_k = vec[0:1, :], vec[1:2, :], vec[2:3, :], vec[3:4, :], vec[4:5, :]
    p_all = p_ref[...]
    pp_all = pp_ref[...] if seq is None else _shift_rows(p_all, pp_ref, first_ref, seq)

    def shifted(c):
        pc = p_all[:, c * dr:(c + 1) * dr]
        return pc + (pp_all[:, c * dr:(c + 1) * dr] - pc) * mu[c:c + 1, :]

    r = shifted(0)
    k = shifted(1)
    v = shifted(2)
    w = -jnp.logaddexp(-(w0 + _dot(tw_ref[...], w2_ref[...])), 0.0) - 0.5
    dec = -jnp.exp(w)
    a = jax.nn.sigmoid(a0 + _dot(ta_ref[...], a2_ref[...]))
    g = _dot(tg_ref[...], g2_ref[...])
    kk = k * k_k
    kmod = k * (1.0 + (a - 1.0) * k_a)
    rk = r * kmod * r_k
    for c in range(dr // LANES):
        sl = slice(c * LANES, (c + 1) * LANES)
        kkc = kk[:, sl]
        kkc = kkc * lax.rsqrt(jnp.maximum(_head_sums(kkc * kkc, bd), 1e-24))
        kk_ref[:, sl] = kkc
        be_ref[:, sl] = kkc * a[:, sl]
        bo_ref[:, sl] = _head_sums(rk[:, sl], bd) * v[:, sl]
    r_ref[...] = r
    dec_ref[...] = dec
    km_ref[...] = kmod
    v_ref[...] = v
    g_ref[...] = g


def _rwkv_prep(p, pprev, tw, ta, tg, w2, a2, g2, mu_rkv, vecs, seq=None):
    m = p.shape[0]
    dr = p.shape[1] // 3
    tm = _tile(m if seq is None else seq, 128, 16)
    row = lambda i: (i, 0)
    full = lambda i: (0, 0)
    outs = [jax.ShapeDtypeStruct((m, dr), F32)] * 8
    if seq is None:
        prev_specs, prev_args = [pl.BlockSpec((tm, 3 * dr), row)], [pprev]
    else:
        prev_specs, prev_args = list(_prev_specs(tm, 3 * dr, seq)), [p, pprev]
    return pl.pallas_call(
        functools.partial(_rwkv_prep_kernel, seq=seq),
        out_shape=outs,
        grid=(m // tm,),
        in_specs=[pl.BlockSpec((tm, 3 * dr), row)] + prev_specs + [
                  pl.BlockSpec((tm, tw.shape[1]), row), pl.BlockSpec((tm, ta.shape[1]), row),
                  pl.BlockSpec((tm, tg.shape[1]), row),
                  pl.BlockSpec(w2.shape, full), pl.BlockSpec(a2.shape, full), pl.BlockSpec(g2.shape, full),
                  pl.BlockSpec(mu_rkv.shape, full), pl.BlockSpec(vecs.shape, full)],
        out_specs=[pl.BlockSpec((tm, dr), row)] * 8,
        compiler_params=_cparams("arbitrary"),
        name="rwkv_prep",
    )(p, *prev_args, tw, ta, tg, w2, a2, g2, mu_rkv, vecs)


SCAN_STEPS = 8


def _segment_ones(width):
    a = lax.broadcasted_iota(jnp.int32, (width, width), 0) // HD_RWKV
    b = lax.broadcasted_iota(jnp.int32, (width, width), 1) // HD_RWKV
    return jnp.where(a == b, 1.0, 0.0).astype(BF16)


def _rwkv_scan_kernel(kk_ref, dec_ref, be_ref, km_ref, r_ref, v_ref, s0_ref, y_ref, s_ref, *, tc, npair):
    c = pl.program_id(1)
    n = HD_RWKV
    gs = min(SCAN_STEPS, tc)
    wide = 2 if npair % 2 == 0 else 1
    assert tc % gs == 0

    @pl.when(c == 0)
    def _():
        s_ref[...] = s0_ref[...]

    seg = _segment_ones(wide * LANES)
    lane = lax.broadcasted_iota(jnp.int32, (n, LANES), 1)
    sub = lax.broadcasted_iota(jnp.int32, (n, LANES), 0)
    diag = (lane % n) == sub

    def stack(parts):
        rows = [jnp.concatenate(parts[i:i + wide], axis=1) for i in range(0, npair, wide)]
        return jnp.concatenate(rows, axis=0).astype(BF16)

    def part(full, p):
        return full[(p // wide) * n:(p // wide + 1) * n, (p % wide) * LANES:(p % wide + 1) * LANES]

    def steps(t0):
        def row(ref, p, r):
            return ref[pl.ds(t0, gs), p * LANES:(p + 1) * LANES][r:r + 1, :]

        y_rows = [[None] * gs for _ in range(npair)]

        def finish(y_all, r):
            for p in range(npair):
                y_rows[p][r] = jnp.sum(jnp.where(diag, part(y_all, p), 0.0), axis=0, keepdims=True)

        pending = None
        for r in range(gs):
            sa_all = _dot(stack([s_ref[0, p] * row(kk_ref, p, r) for p in range(npair)]), seg)
            vb_all = _dot(stack([jnp.where(diag, row(v_ref, p, r), 0.0) for p in range(npair)]), seg)
            if pending is not None:
                finish(pending, r - 1)
            ys = []
            for p in range(npair):
                s_new = (s_ref[0, p] * jnp.exp(row(dec_ref, p, r)) - part(sa_all, p) * row(be_ref, p, r)
                         + part(vb_all, p) * row(km_ref, p, r))
                s_ref[0, p] = s_new
                ys.append(s_new * row(r_ref, p, r))
            pending = _dot(stack(ys), seg)
        finish(pending, gs - 1)
        for p in range(npair):
            y_ref[pl.ds(t0, gs), p * LANES:(p + 1) * LANES] = jnp.concatenate(y_rows[p], axis=0)

    if tc == gs:
        steps(0)
    else:
        def body(i, carry):
            steps(pl.multiple_of(i * gs, gs))
            return carry

        lax.fori_loop(0, tc // gs, body, 0)


def _rwkv_scan(kk, dec, be, km, r, v, s0, batch, seq):
    m, dr = kk.shape
    npair = dr // LANES
    tc = _tile(seq, 128, 8)
    nc = seq // tc
    if seq == 1:
        kk, dec, be, km, r, v = (a.reshape(m, 1, dr) for a in (kk, dec, be, km, r, v))
        op = pl.BlockSpec((None, 1, dr), lambda b, c: (b, 0, 0))
        y_shape = jax.ShapeDtypeStruct((m, 1, dr), F32)
    else:
        op = pl.BlockSpec((tc, dr), lambda b, c: (b * nc + c, 0))
        y_shape = jax.ShapeDtypeStruct((m, dr), F32)
    kern = functools.partial(_rwkv_scan_kernel, tc=tc, npair=npair)
    st = pl.BlockSpec((1, npair, HD_RWKV, LANES), lambda b, c: (b, 0, 0, 0))
    y, s_fin = pl.pallas_call(
        kern,
        out_shape=[y_shape, jax.ShapeDtypeStruct((batch, npair, HD_RWKV, LANES), F32)],
        grid=(batch, nc),
        in_specs=[op] * 6 + [st],
        out_specs=[op, st],
        compiler_params=_cparams("arbitrary", "arbitrary"),
        name="rwkv_scan",
    )(kk, dec, be, km, r, v, s0)
    return y.reshape(m, dr), s_fin


CHUNK = 64


def _rwkv_chunk_kernel(kk_ref, ld_ref, be_ref, km_ref, r_ref, v_ref, s0_ref, y_ref, s_ref, *, npair):
    c = pl.program_id(1)
    L = CHUNK
    n = HD_RWKV

    @pl.when(c == 0)
    def _():
        s_ref[...] = s0_ref[...]

    lane = lax.broadcasted_iota(jnp.int32, (L, LANES), 1)
    trow = lax.broadcasted_iota(jnp.int32, (L, LANES), 0)
    head0 = lane < n
    src = lane % n
    strict = trow > src
    incl = trow >= src
    tri = (lax.broadcasted_iota(jnp.int32, (L, L), 0) >= lax.broadcasted_iota(jnp.int32, (L, L), 1))
    tri = jnp.where(tri, 1.0, 0.0).astype(BF16)
    bd_r = lax.broadcasted_iota(jnp.int32, (LANES, LANES), 0) // n
    bd_c = lax.broadcasted_iota(jnp.int32, (LANES, LANES), 1) // n
    bdmask = bd_r == bd_c

    def blk(x):
        return jnp.concatenate([jnp.where(head0, x, 0.0), jnp.where(head0, 0.0, x)], axis=0).astype(BF16)

    def bf(x):
        return x.astype(BF16)

    pairs = range(npair)
    sls = [slice(p * LANES, (p + 1) * LANES) for p in pairs]

    cum = []
    for p in pairs:
        ld = ld_ref[:, sls[p]]
        hi = bf(ld)
        r1 = ld - hi.astype(F32)
        mid = bf(r1)
        lo = bf(r1 - mid.astype(F32))
        cum.append(_dot(tri, hi) + _dot(tri, mid) + _dot(tri, lo))

    z, s0t = [], []
    for p in pairs:
        e_neg = jnp.exp(-cum[p])
        abar = -kk_ref[:, sls[p]] * jnp.exp(cum[p] - ld_ref[:, sls[p]])
        rbar = r_ref[:, sls[p]] * jnp.exp(cum[p])
        lhs = bf(jnp.concatenate([abar, rbar], axis=0))
        rhs = jnp.concatenate([blk(be_ref[:, sls[p]] * e_neg), blk(km_ref[:, sls[p]] * e_neg)], axis=0)
        z.append(_dot_nt(lhs, rhs))
        s0t.append(_dot_nt(lhs, bf(s_ref[0, p])))

    u, pw = [], []
    for p in pairs:
        amat = jnp.where(strict, z[p][:L, 2 * L:], 0.0)
        u.append(s0t[p][:L] + _dot(bf(amat), blk(v_ref[:, sls[p]])))
        pw.append(jnp.where(strict, z[p][:L, :2 * L], 0.0))

    n_fac = L.bit_length() - 1
    for it in range(n_fac):
        u = [u[p] + _dot(bf(pw[p]), blk(u[p])) for p in pairs]
        if it + 1 < n_fac:
            pw = [_dot(bf(pw[p]), blk(pw[p])) for p in pairs]

    for p in pairs:
        rbk = jnp.concatenate([jnp.where(incl, z[p][L:, :2 * L], 0.0), jnp.where(incl, z[p][L:, 2 * L:], 0.0)],
                              axis=1)
        uv = jnp.concatenate([blk(u[p]), blk(v_ref[:, sls[p]])], axis=0)
        y_ref[:, sls[p]] = s0t[p][L:] + _dot(bf(rbk), uv)

    for p in pairs:
        cl = cum[p][L - 1:L, :]
        e_end = jnp.exp(cl - cum[p])
        uv_t = jnp.transpose(jnp.concatenate([u[p], v_ref[:, sls[p]]], axis=0))
        ends = jnp.concatenate([be_ref[:, sls[p]] * e_end, km_ref[:, sls[p]] * e_end], axis=0)
        cmat = _dot(bf(uv_t), bf(ends))
        s_ref[0, p] = s_ref[0, p] * jnp.exp(cl) + jnp.where(bdmask, cmat, 0.0)


def _rwkv_chunked(kk, ld, be, km, r, v, s0, batch, seq):
    m, dr = kk.shape
    npair = dr // LANES
    assert seq % CHUNK == 0 and 2 * CHUNK == LANES
    nc = seq // CHUNK
    op = pl.BlockSpec((CHUNK, dr), lambda b, c: (b * nc + c, 0))
    st = pl.BlockSpec((1, npair, LANES, LANES), lambda b, c: (b, 0, 0, 0))
    return pl.pallas_call(
        functools.partial(_rwkv_chunk_kernel, npair=npair),
        out_shape=[jax.ShapeDtypeStruct((m, dr), F32), jax.ShapeDtypeStruct((batch, npair, LANES, LANES), F32)],
        grid=(batch, nc),
        in_specs=[op] * 6 + [st],
        out_specs=[op, st],
        compiler_params=_cparams("arbitrary", "arbitrary"),
        name="rwkv_chunked",
    )(kk, ld, be, km, r, v, s0)


def _rwkv_post_kernel(y_ref, bo_ref, g_ref, ln_ref, o_ref):
    bd = _block_diag_ones()
    ln = ln_ref[...]
    inv_n = 1.0 / HD_RWKV
    for c in range(y_ref.shape[1] // LANES):
        sl = slice(c * LANES, (c + 1) * LANES)
        y = y_ref[:, sl]
        mean = _head_sums(y, bd) * inv_n
        d = y - mean
        var = _head_sums(d * d, bd) * inv_n
        yn = d * lax.rsqrt(var + GN_EPS) * ln[0:1, sl] + ln[1:2, sl]
        o_ref[:, sl] = ((yn + bo_ref[:, sl]) * g_ref[:, sl]).astype(BF16)


def _rwkv_post(y, bonus, g, ln):
    m, dr = y.shape
    tm = _tile(m, 256, 16)
    row = lambda i: (i, 0)
    return pl.pallas_call(
        _rwkv_post_kernel,
        out_shape=jax.ShapeDtypeStruct((m, dr), BF16),
        grid=(m // tm,),
        in_specs=[pl.BlockSpec((tm, dr), row)] * 3 + [pl.BlockSpec(ln.shape, lambda i: (0, 0))],
        out_specs=pl.BlockSpec((tm, dr), row),
        compiler_params=_cparams("arbitrary"),
        name="rwkv_post",
    )(y, bonus, g, ln)


def _pack_state(s):
    b, h, n, _ = s.shape
    return s.reshape(b, h // 2, 2, n, n).transpose(0, 1, 3, 2, 4).reshape(b, h // 2, n, 2 * n)


def _unpack_state(s):
    b, p, n, _ = s.shape
    return s.reshape(b, p, n, 2, n).transpose(0, 1, 3, 2, 4).reshape(b, 2 * p, n, n)


def _blockdiag_state(s):
    b, h, n, _ = s.shape
    s = s.reshape(b, h // 2, 2, n, n)
    z = jnp.zeros_like(s[:, :, 0])
    top = jnp.concatenate([s[:, :, 0], z], axis=-1)
    bot = jnp.concatenate([z, s[:, :, 1]], axis=-1)
    return jnp.concatenate([top, bot], axis=-2)


def _blockdiag_to_heads(s):
    b, p, _, _ = s.shape
    n = HD_RWKV
    return jnp.stack([s[:, :, :n, :n], s[:, :, n:, n:]], axis=2).reshape(b, 2 * p, n, n)


def _pad_cols(w, n):
    return jnp.pad(w, ((0, 0), (0, n - w.shape[1])))


def _pad_rows(w, n):
    return jnp.pad(w, ((0, n - w.shape[0]), (0, 0)))


def _rope_tables(pos, rows_per_pos):
    half = HD_ATT // 2
    inv = ROPE_THETA ** (-jnp.arange(half, dtype=F32) / half)
    ang = pos.astype(F32)[:, None] * inv[None, :]
    cos = jnp.cos(ang)
    sin = jnp.sin(ang)
    cos2 = jnp.concatenate([cos, cos], axis=-1)
    sin2 = jnp.concatenate([-sin, sin], axis=-1)
    if rows_per_pos is not None:
        cos2 = jnp.broadcast_to(cos2, (rows_per_pos, HD_ATT))
        sin2 = jnp.broadcast_to(sin2, (rows_per_pos, HD_ATT))
    return cos2, sin2


def kernel(x_prompt, x_sample, cache_k, cache_v, state_shift, state_wkv, page_table, c_prompt, c_sample, g_norm, w_ada, b_ada, ffn_w_gate, ffn_w_up, ffn_w_down, w_in, w_out, mu_rkv, mu_wag, w0, w_lora1, w_lora2, a0, a_lora1, a_lora2, g_lora1, g_lora2, k_k, k_a, r_k, ln_x_w, ln_x_b, g_final):
    B, T, D = x_prompt.shape
    DB, DS, _ = x_sample.shape
    depth = g_norm.shape[0]
    assert DS == 1
    n_pages = page_table.shape[1]
    page = cache_k.shape[2]
    past_len = n_pages * page
    assert past_len % MOBA_BLOCK == 0 and MOBA_BLOCK == 2 * page
    da = cache_k.shape[3] * cache_k.shape[4]
    nh = da // HD_ATT
    dr = D - da
    n_slots = min(MOBA_TOPK, past_len // MOBA_BLOCK)

    xp = x_prompt.reshape(B * T, D)
    xs = x_sample.reshape(DB, D)
    pt_flat = page_table.reshape(-1).astype(jnp.int32)
    cos_p, sin_p = _rope_tables(jnp.arange(T, dtype=jnp.int32), None)
    cos_p = jnp.tile(cos_p, (B, 1))
    sin_p = jnp.tile(sin_p, (B, 1))
    cos_s, sin_s = _rope_tables(jnp.full((1,), past_len, dtype=jnp.int32), DB)

    outs = [[] for _ in range(8)]
    n_c = B + DB
    n_cp = -(-n_c // 8) * 8
    c_all = jnp.pad(jnp.concatenate([c_prompt, c_sample], axis=0), ((0, n_cp - n_c), (0, 0)))

    for l in range(depth):
        bf = lambda w: w.astype(BF16)
        wout = bf(w_out[l])
        lw = -(-w_lora1.shape[2] // LANES) * LANES
        la = -(-a_lora1.shape[2] // LANES) * LANES
        lg = -(-g_lora1.shape[2] // LANES) * LANES
        w1, a1, g1 = bf(_pad_cols(w_lora1[l], lw)), bf(_pad_cols(a_lora1[l], la)), bf(_pad_cols(g_lora1[l], lg))
        w2, a2, g2 = bf(_pad_rows(w_lora2[l], lw)), bf(_pad_rows(a_lora2[l], la)), bf(_pad_rows(g_lora2[l], lg))
        vecs = jnp.stack([w0[l], a0[l], k_k[l], k_a[l], r_k[l].reshape(dr)], axis=0)
        vecs = jnp.pad(vecs, ((0, 3), (0, 0)))
        ln = jnp.pad(jnp.stack([ln_x_w[l], ln_x_b[l]], axis=0), ((0, 6), (0, 0)))
        mu3 = jnp.pad(mu_rkv[l], ((0, 5), (0, 0)))
        muw = jnp.pad(mu_wag[l], ((0, 5), (0, 0)))

        mod = _ada(c_all, w_ada[l], b_ada[l])
        chunks = [mod[:, k * D:(k + 1) * D] for k in range(9)]
        mod_p = [ch[:B].reshape(B, 1, D) for ch in chunks]
        mod_s = [ch[B:B + DB].reshape(1, DB, D) for ch in chunks]

        wd_bf = {}

        def ffn(x, mods, rpg, idx):
            sub = 2 * idx
            sh, sc, ga = mods[3 * sub], mods[3 * sub + 1], mods[3 * sub + 2]
            (h,) = _modulate(x, g_norm[l, sub], sh, sc, rpg, False)
            if idx in wd_bf:
                u = _ffn_up(h, ffn_w_gate, ffn_w_up, l, idx)
            else:
                u, wd_bf[idx] = _ffn_up(h, ffn_w_gate, ffn_w_up, l, idx, ffn_w_down)
            return _ffn_down(u, wd_bf[idx], x, ga, rpg)

        xp = ffn(xp, mod_p, T, 0)
        hb, hf = _modulate(xp, g_norm[l, 1], mod_p[3], mod_p[4], T, True)
        q = _proj(hb, w_in, l, 0, da, (cos_p, sin_p))
        k = _proj(hb, w_in, l, da, da, (cos_p, sin_p))
        v = _proj(hb, w_in, l, 2 * da, da)
        prk = _proj(hb, w_in, l, 3 * da, 3 * dr)
        att = _moba_prompt(q, k, v, B, T)

        xs = ffn(xs, mod_s, 1, 0)
        hbs, hfs = _modulate(xs, g_norm[l, 1], mod_s[3], mod_s[4], 1, True)
        qs = _proj(hbs, w_in, l, 0, da, (cos_s, sin_s))
        ks = _proj(hbs, w_in, l, da, da, (cos_s, sin_s))
        vs = _proj(hbs, w_in, l, 2 * da, da)
        n_prev = DB + B
        n_prev_p = -(-n_prev // 16) * 16
        prev_rows = jnp.concatenate([state_shift[l], jnp.zeros((B, D), F32),
                                     jnp.zeros((n_prev_p - n_prev, D), F32)], axis=0)
        p_first = _proj(bf(prev_rows), w_in, l, 3 * da, 3 * dr)
        prk_s = _proj(hbs, w_in, l, 3 * da, 3 * dr)

        h3 = hf.reshape(B, T, D)

        def rwkv(hf_, hprev_, prk_, pprev_, s0, nb_, nt_):
            seq = None if nt_ == 1 else nt_
            tw, ta, tg = _lora1(hf_, hprev_, muw, w1, a1, g1, seq)
            r_, dec_, km_, v_, kk_, be_, g_, bo_ = _rwkv_prep(prk_, pprev_, tw, ta, tg, w2, a2, g2, mu3, vecs, seq)
            if nt_ % CHUNK == 0:
                y, s_fin = _rwkv_chunked(kk_, dec_, be_, km_, r_, v_, _blockdiag_state(s0), nb_, nt_)
                s_fin = _blockdiag_to_heads(s_fin)
            else:
                y, s_fin = _rwkv_scan(kk_, dec_, be_, km_, r_, v_, _pack_state(s0), nb_, nt_)
                s_fin = _unpack_state(s_fin)
            rw = _rwkv_post(y, bo_, g_, ln)
            return rw, s_fin

        s0_p = jnp.zeros((B, dr // HD_RWKV, HD_RWKV, HD_RWKV), F32)
        rw_p, sfin_p = rwkv(hf, jnp.zeros((B, 1, D), F32), prk, p_first[DB:DB + B].reshape(B, 1, 3 * dr),
                            s0_p, B, T)
        xp = _out_proj(att, rw_p, wout, xp, mod_p[5], T)
        xp = ffn(xp, mod_p, T, 1)

        n_blocks = past_len // MOBA_BLOCK
        kmeans = _block_means(cache_k, l, pt_flat)
        sel = _select_blocks(kmeans, qs, DB, n_blocks, nh, n_slots)
        att_s = _moba_sample(qs, ks, vs, cache_k, cache_v, l, pt_flat, sel.reshape(-1), DB, n_pages, n_slots)

        rw_s, sfin_s = rwkv(hfs, state_shift[l], prk_s, p_first[:DB], state_wkv[l], DB, 1)
        xs = _out_proj(att_s, rw_s, wout, xs, mod_s[5], 1)
        xs = ffn(xs, mod_s, 1, 1)

        outs[0].append(k.reshape(B, T, nh, HD_ATT))
        outs[1].append(v.reshape(B, T, nh, HD_ATT))
        outs[2].append(h3[:, -1])
        outs[3].append(sfin_p)
        outs[4].append(ks.reshape(DB, 1, nh, HD_ATT))
        outs[5].append(vs.reshape(DB, 1, nh, HD_ATT))
        outs[6].append(hfs)
        outs[7].append(sfin_s)

    y_prompt = _rmsnorm(xp, g_final).reshape(B, T, D)
    y_sample = _rmsnorm(xs, g_final).reshape(DB, 1, D)
    return (y_prompt, y_sample) + tuple(jnp.stack(o) for o in outs)
```

```python
import functools

import jax
import jax.numpy as jnp
from jax import lax
from jax.experimental import pallas as pl
from jax.experimental.pallas import tpu as pltpu

F32 = jnp.float32
BF16 = jnp.bfloat16

HD_ATT = 128
HD_RWKV = 64
LANES = 128
MOBA_BLOCK = 256
MOBA_TOPK = 3
ROPE_THETA = 10000.0
RMS_EPS = 1e-6
GN_EPS = 64e-5
NEG_BIG = -1e30
VMEM_LIMIT = 56 * 1024 * 1024


def _cparams(*sem):
    return pltpu.CompilerParams(dimension_semantics=sem, vmem_limit_bytes=VMEM_LIMIT)


def _tile(n, pref, align):
    if n <= pref:
        return n
    t = (pref // align) * align
    while t >= align:
        if n % t == 0:
            return t
        t -= align
    return n


def _dot(a, b):
    return jnp.dot(a, b, preferred_element_type=F32)


def _dot_nt(a, b):
    return lax.dot_general(a, b, (((1,), (1,)), ((), ())), preferred_element_type=F32)


def _ada_kernel(c_ref, w_ref, b_ref, o_ref):
    c = c_ref[...]
    s = (c * jax.nn.sigmoid(c)).astype(BF16)
    o_ref[...] = _dot(s, w_ref[...].astype(BF16)) + b_ref[...]


def _ada(c, w, b):
    m, d = c.shape
    n = w.shape[1]
    tn = _tile(n, 512, LANES)
    return pl.pallas_call(
        _ada_kernel,
        out_shape=jax.ShapeDtypeStruct((m, n), F32),
        grid=(n // tn,),
        in_specs=[pl.BlockSpec((m, d), lambda j: (0, 0)),
                  pl.BlockSpec((d, tn), lambda j: (0, j)),
                  pl.BlockSpec((1, tn), lambda j: (0, j))],
        out_specs=pl.BlockSpec((m, tn), lambda j: (0, j)),
        compiler_params=_cparams("arbitrary"),
        name="ada_proj",
    )(c, w, b.reshape(1, n))


def _modulate_kernel(x_ref, g_ref, sh_ref, sc_ref, *o_refs):
    x = x_ref[...]
    y = x * lax.rsqrt(jnp.mean(x * x, axis=-1, keepdims=True) + RMS_EPS) * g_ref[...]
    h = y * (1.0 + sc_ref[0]) + sh_ref[0]
    o_refs[0][...] = h.astype(BF16)
    if len(o_refs) > 1:
        o_refs[1][...] = h


def _modulate(x, g, sh, sc, rows_per_group, want_f32):
    m, d = x.shape
    r = sh.shape[1]
    tm = m if r > 1 else _tile(rows_per_group, 256, 16)
    grp = (lambda i: (0, 0, 0)) if r > 1 else (lambda i: (i * tm // rows_per_group, 0, 0))
    out_shape = [jax.ShapeDtypeStruct((m, d), BF16)]
    out_specs = [pl.BlockSpec((tm, d), lambda i: (i, 0))]
    if want_f32:
        out_shape.append(jax.ShapeDtypeStruct((m, d), F32))
        out_specs.append(pl.BlockSpec((tm, d), lambda i: (i, 0)))
    return pl.pallas_call(
        _modulate_kernel,
        out_shape=out_shape,
        grid=(m // tm,),
        in_specs=[pl.BlockSpec((tm, d), lambda i: (i, 0)),
                  pl.BlockSpec((1, d), lambda i: (0, 0)),
                  pl.BlockSpec((1, r, d), grp),
                  pl.BlockSpec((1, r, d), grp)],
        out_specs=out_specs,
        compiler_params=_cparams("arbitrary"),
        name="modulate",
    )(x, g.reshape(1, d), sh, sc)


def _rmsnorm_kernel(x_ref, g_ref, o_ref):
    x = x_ref[...]
    o_ref[...] = x * lax.rsqrt(jnp.mean(x * x, axis=-1, keepdims=True) + RMS_EPS) * g_ref[...]


def _rmsnorm(x, g):
    m, d = x.shape
    tm = _tile(m, 256, 8)
    return pl.pallas_call(
        _rmsnorm_kernel,
        out_shape=jax.ShapeDtypeStruct((m, d), F32),
        grid=(m // tm,),
        in_specs=[pl.BlockSpec((tm, d), lambda i: (i, 0)), pl.BlockSpec((1, d), lambda i: (0, 0))],
        out_specs=pl.BlockSpec((tm, d), lambda i: (i, 0)),
        compiler_params=_cparams("arbitrary"),
        name="final_rmsnorm",
    )(x, g.reshape(1, d))


def _first_pass(nj):
    return lambda i, j: jnp.where(i == 0, j, nj - 1)


def _ffn_up_kernel(h_ref, wg_ref, wu_ref, *rest, has_wd, has_rider):
    rest = list(rest)
    wd_ref = rest.pop(0) if has_wd else None
    hs_ref = rest.pop(0) if has_rider else None
    o_ref = rest.pop(0)
    wdb_ref = rest.pop(0) if has_wd else None
    us_ref = rest.pop(0) if has_rider else None
    wg = wg_ref[...].astype(BF16)
    wu = wu_ref[...].astype(BF16)

    def gated(h):
        g = _dot(h, wg)
        return (g * jax.nn.sigmoid(g) * _dot(h, wu)).astype(BF16)

    if has_wd or has_rider:
        @pl.when(pl.program_id(0) == 0)
        def _():
            if has_wd:
                wdb_ref[...] = wd_ref[...].astype(BF16)
            if has_rider:
                us_ref[...] = gated(hs_ref[...])

    tm = h_ref.shape[0]
    rows = min(tm, 512)
    for r0 in range(0, tm, rows):
        o_ref[r0:r0 + rows, :] = gated(h_ref[r0:r0 + rows, :])


def _ffn_up(h, wg_all, wu_all, layer, idx, wd_all=None, rider=None):
    m, d = h.shape
    f = wg_all.shape[-1]
    tm = _tile(m, 2048, 16)
    tn = _tile(f, 256, LANES)
    nj = f // tn
    once = pl.Buffered(1)
    fp = _first_pass(nj)
    w_spec = pl.BlockSpec((None, None, d, tn), lambda i, j: (layer, idx, 0, j))
    in_specs = [pl.BlockSpec((tm, d), lambda i, j: (i, 0), pipeline_mode=once), w_spec, w_spec]
    out_shape = [jax.ShapeDtypeStruct((m, f), BF16)]
    out_specs = [pl.BlockSpec((tm, tn), lambda i, j: (i, j))]
    args = [h, wg_all, wu_all]
    if wd_all is not None:
        in_specs.append(pl.BlockSpec((None, None, tn, d), lambda i, j: (layer, idx, fp(i, j), 0),
                                     pipeline_mode=once))
        out_shape.append(jax.ShapeDtypeStruct((f, d), BF16))
        out_specs.append(pl.BlockSpec((tn, d), lambda i, j: (fp(i, j), 0)))
        args.append(wd_all)
    if rider is not None:
        ms = rider.shape[0]
        in_specs.append(pl.BlockSpec((ms, d), lambda i, j: (0, 0)))
        out_shape.append(jax.ShapeDtypeStruct((ms, f), BF16))
        out_specs.append(pl.BlockSpec((ms, tn), lambda i, j: (0, fp(i, j))))
        args.append(rider)
    res = pl.pallas_call(
        functools.partial(_ffn_up_kernel, has_wd=wd_all is not None, has_rider=rider is not None),
        out_shape=out_shape,
        grid=(m // tm, nj),
        in_specs=in_specs,
        out_specs=out_specs,
        compiler_params=_cparams("arbitrary", "arbitrary"),
        name="ffn_up",
    )(*args)
    return res if len(res) > 1 else res[0]


def _ffn_down_kernel(u_ref, w_ref, x_ref, ga_ref, *rest):
    w = w_ref[...]
    if len(rest) == 1:
        (o_ref,) = rest
    else:
        us_ref, xs_ref, gas_ref, o_ref, os_ref = rest

        @pl.when(pl.program_id(0) == 0)
        def _():
            os_ref[...] = xs_ref[...] + 0.5 * gas_ref[...] * _dot(us_ref[...], w)

    o_ref[...] = x_ref[...] + 0.5 * ga_ref[0] * _dot(u_ref[...], w)


def _ffn_down(u, wd, x, ga, rows_per_group, rider=None):
    m, f = u.shape
    d = wd.shape[1]
    r = ga.shape[1]
    tm = m if r > 1 else _tile(rows_per_group, 1024, 16)
    tn = _tile(d, 256, LANES)
    nj = d // tn
    grp = (lambda i, j: (0, 0, j)) if r > 1 else (lambda i, j: (i * tm // rows_per_group, 0, j))
    in_specs = [pl.BlockSpec((tm, f), lambda i, j: (i, 0), pipeline_mode=pl.Buffered(1)),
                pl.BlockSpec((f, tn), lambda i, j: (0, j)),
                pl.BlockSpec((tm, tn), lambda i, j: (i, j)),
                pl.BlockSpec((1, r, tn), grp)]
    out_shape = [jax.ShapeDtypeStruct((m, d), F32)]
    out_specs = [pl.BlockSpec((tm, tn), lambda i, j: (i, j))]
    args = [u, wd, x, ga]
    if rider is not None:
        ms = rider[0].shape[0]
        fp = _first_pass(nj)
        tile_s = pl.BlockSpec((ms, tn), lambda i, j: (0, fp(i, j)))
        in_specs += [pl.BlockSpec((ms, f), lambda i, j: (0, 0)), tile_s, tile_s]
        out_shape.append(jax.ShapeDtypeStruct((ms, d), F32))
        out_specs.append(tile_s)
        args += list(rider)
    res = pl.pallas_call(
        _ffn_down_kernel,
        out_shape=out_shape,
        grid=(m // tm, nj),
        in_specs=in_specs,
        out_specs=out_specs,
        compiler_params=_cparams("arbitrary", "arbitrary"),
        name="ffn_down",
    )(*args)
    return res if rider is not None else res[0]


def _rope_store(o_ref, x, cos, sin):
    for c in range(x.shape[1] // HD_ATT):
        xc = x[:, c * HD_ATT:(c + 1) * HD_ATT]
        o_ref[:, c * HD_ATT:(c + 1) * HD_ATT] = xc * cos + pltpu.roll(xc, HD_ATT // 2, 1) * sin


def _proj_kernel(h_ref, w_ref, *rest, has_rope, has_rider):
    rest = list(rest)
    cos_ref, sin_ref = (rest.pop(0), rest.pop(0)) if has_rope else (None, None)
    hs_ref = rest.pop(0) if has_rider else None
    cs_ref, ss_ref = (rest.pop(0), rest.pop(0)) if has_rope and has_rider else (None, None)
    o_ref = rest.pop(0)
    os_ref = rest.pop(0) if has_rider else None
    w = w_ref[...].astype(BF16)

    def emit(out_ref, rows, cos_r, sin_r):
        x = _dot(rows, w)
        if has_rope:
            _rope_store(out_ref, x, cos_r[...], sin_r[...])
        else:
            out_ref[...] = x

    if has_rider:
        @pl.when(pl.program_id(0) == 0)
        def _():
            emit(os_ref, hs_ref[...], cs_ref, ss_ref)

    emit(o_ref, h_ref[...], cos_ref, sin_ref)


def _proj(h, w, layer, col0, ncols, rope=None, rider=None, rider_rope=None):
    m, d = h.shape
    tm = _tile(m, 1024, 16)
    tn = _tile(ncols, 512, LANES)
    assert col0 % tn == 0
    j0 = col0 // tn
    nj = ncols // tn
    in_specs = [pl.BlockSpec((tm, d), lambda i, j: (i, 0)),
                pl.BlockSpec((None, d, tn), lambda i, j: (layer, 0, j + j0))]
    args = [h, w]
    out_shape = [jax.ShapeDtypeStruct((m, ncols), F32)]
    out_specs = [pl.BlockSpec((tm, tn), lambda i, j: (i, j))]
    if rope is not None:
        in_specs += [pl.BlockSpec((tm, HD_ATT), lambda i, j: (i, 0))] * 2
        args += list(rope)
    if rider is not None:
        ms = rider.shape[0]
        fp = _first_pass(nj)
        in_specs.append(pl.BlockSpec((ms, d), lambda i, j: (0, 0)))
        args.append(rider)
        if rope is not None:
            in_specs += [pl.BlockSpec((ms, HD_ATT), lambda i, j: (0, 0))] * 2
            args += list(rider_rope)
        out_shape.append(jax.ShapeDtypeStruct((ms, ncols), F32))
        out_specs.append(pl.BlockSpec((ms, tn), lambda i, j: (0, fp(i, j))))
    res = pl.pallas_call(
        functools.partial(_proj_kernel, has_rope=rope is not None, has_rider=rider is not None),
        out_shape=out_shape,
        grid=(m // tm, nj),
        in_specs=in_specs,
        out_specs=out_specs,
        compiler_params=_cparams("arbitrary", "arbitrary"),
        name="in_proj_rope" if rope is not None else "in_proj",
    )(*args)
    return res if rider is not None else res[0]


def _out_proj_kernel(att_ref, rw_ref, wa_ref, wr_ref, x_ref, ga_ref, o_ref):
    y = _dot(att_ref[...], wa_ref[...]) + _dot(rw_ref[...], wr_ref[...])
    o_ref[...] = x_ref[...] + ga_ref[0] * y


def _out_proj(att, rw, w, x, ga, rows_per_group):
    m, da = att.shape
    dr = rw.shape[1]
    d = w.shape[1]
    assert da == dr
    r = ga.shape[1]
    tm = m if r > 1 else _tile(rows_per_group, 1024, 16)
    tn = _tile(d, 512, LANES)
    grp = (lambda i, j: (0, 0, j)) if r > 1 else (lambda i, j: (i * tm // rows_per_group, 0, j))
    return pl.pallas_call(
        _out_proj_kernel,
        out_shape=jax.ShapeDtypeStruct((m, d), F32),
        grid=(m // tm, d // tn),
        in_specs=[pl.BlockSpec((tm, da), lambda i, j: (i, 0)),
                  pl.BlockSpec((tm, dr), lambda i, j: (i, 0)),
                  pl.BlockSpec((da, tn), lambda i, j: (0, j)),
                  pl.BlockSpec((dr, tn), lambda i, j: (1, j)),
                  pl.BlockSpec((tm, tn), lambda i, j: (i, j)),
                  pl.BlockSpec((1, r, tn), grp)],
        out_specs=pl.BlockSpec((tm, tn), lambda i, j: (i, j)),
        compiler_params=_cparams("arbitrary", "arbitrary"),
        name="out_proj",
    )(att, rw, w, w, x, ga)


def _rank_select(g_rows, n_valid, topk):
    nb = len(g_rows)
    sel = []
    for j in range(nb):
        rank = jnp.zeros_like(g_rows[j])
        for n in range(nb):
            if n == j:
                continue
            beats = (g_rows[n] > g_rows[j]) if n > j else (g_rows[n] >= g_rows[j])
            rank = rank + jnp.where(beats, jnp.where(n < n_valid, 1.0, 0.0), 0.0)
        sel.append(rank < float(topk))
    return sel


MOBA_HEADS_PER_STEP = 4


def _moba_prompt_kernel(q_ref, k_ref, v_ref, o_ref, kb_ref, vt_ref, km_ref, m_ref, l_ref, acc_ref,
                        *, nb, scale, hp):
    i = pl.program_id(2)
    blk = MOBA_BLOCK
    heads = [slice(hh * HD_ATT, (hh + 1) * HD_ATT) for hh in range(hp)]

    @pl.when(i == 0)
    def _():
        km_ref[...] = jnp.zeros_like(km_ref)
        for hh in range(hp):
            for j in range(nb):
                kj = k_ref[j * blk:(j + 1) * blk, heads[hh]]
                km_ref[hh, j:j + 1, :] = jnp.mean(kj, axis=0, keepdims=True)
                kb_ref[hh, j] = kj.astype(BF16)
                vt_ref[hh, j] = jnp.transpose(v_ref[j * blk:(j + 1) * blk, heads[hh]]).astype(BF16)

    kpos = lax.broadcasted_iota(jnp.int32, (blk, blk), 0)
    qpos = lax.broadcasted_iota(jnp.int32, (blk, blk), 1)
    qb, sel = [], []
    for hh in range(hp):
        q = q_ref[:, heads[hh]]
        gt = _dot_nt(km_ref[hh].astype(BF16), q.astype(BF16))
        sel.append(_rank_select([gt[j:j + 1, :] for j in range(nb)], i, MOBA_TOPK))
        qb.append((q * scale).astype(BF16))

    for hh in range(hp):
        s = jnp.where(kpos <= qpos, _dot_nt(kb_ref[hh, i], qb[hh]), NEG_BIG)
        m0 = jnp.max(s, axis=0, keepdims=True)
        p = jnp.exp(s - m0)
        m_ref[hh] = m0
        l_ref[hh] = jnp.sum(p, axis=0, keepdims=True)
        acc_ref[hh] = _dot(vt_ref[hh, i], p.astype(BF16))

    for j in range(nb - 1):
        @pl.when(j < i)
        def _(j=j):
            for hh in range(hp):
                s = jnp.where(sel[hh][j], _dot_nt(kb_ref[hh, j], qb[hh]), NEG_BIG)
                m_old = m_ref[hh]
                m_new = jnp.maximum(m_old, jnp.max(s, axis=0, keepdims=True))
                alpha = jnp.exp(m_old - m_new)
                p = jnp.exp(s - m_new)
                m_ref[hh] = m_new
                l_ref[hh] = alpha * l_ref[hh] + jnp.sum(p, axis=0, keepdims=True)
                acc_ref[hh] = alpha * acc_ref[hh] + _dot(vt_ref[hh, j], p.astype(BF16))

    for hh in range(hp):
        o_ref[:, heads[hh]] = jnp.transpose(acc_ref[hh] / l_ref[hh]).astype(BF16)


def _moba_prompt(q, k, v, batch, seq):
    m, da = q.shape
    nh = da // HD_ATT
    assert seq % MOBA_BLOCK == 0
    nb = seq // MOBA_BLOCK
    nbp = -(-nb // 8) * 8
    hp = _tile(nh, MOBA_HEADS_PER_STEP, 1)
    w = hp * HD_ATT
    kern = functools.partial(_moba_prompt_kernel, nb=nb, scale=HD_ATT ** -0.5, hp=hp)
    return pl.pallas_call(
        kern,
        out_shape=jax.ShapeDtypeStruct((m, da), BF16),
        grid=(batch, nh // hp, nb),
        in_specs=[pl.BlockSpec((MOBA_BLOCK, w), lambda b, h, i: (b * nb + i, h)),
                  pl.BlockSpec((seq, w), lambda b, h, i: (b, h)),
                  pl.BlockSpec((seq, w), lambda b, h, i: (b, h))],
        out_specs=pl.BlockSpec((MOBA_BLOCK, w), lambda b, h, i: (b * nb + i, h)),
        scratch_shapes=[pltpu.VMEM((hp, nb, MOBA_BLOCK, HD_ATT), BF16),
                        pltpu.VMEM((hp, nb, HD_ATT, MOBA_BLOCK), BF16),
                        pltpu.VMEM((hp, nbp, HD_ATT), F32),
                        pltpu.VMEM((hp, 1, MOBA_BLOCK), F32),
                        pltpu.VMEM((hp, 1, MOBA_BLOCK), F32),
                        pltpu.VMEM((hp, HD_ATT, MOBA_BLOCK), F32)],
        compiler_params=_cparams("arbitrary", "arbitrary", "arbitrary"),
        name="moba_prompt",
    )(q, k, v)


MEAN_BLOCKS_PER_STEP = 4


def _block_mean_kernel(pt_ref, *refs):
    o_ref = refs[-1]
    for n in range(len(refs) // 2):
        tot = jnp.sum(refs[2 * n][...], axis=0) + jnp.sum(refs[2 * n + 1][...], axis=0)
        o_ref[n] = tot * (1.0 / MOBA_BLOCK)


def _block_means(cache_k, layer, page_table_flat):
    _, _, page, nh, hd = cache_k.shape
    n = page_table_flat.shape[0] // 2
    per = _tile(n, MEAN_BLOCKS_PER_STEP, 1)

    def page_map(e):
        return lambda s, pt: (layer, pt[2 * per * s + e], 0, 0, 0)

    return pl.pallas_call(
        _block_mean_kernel,
        out_shape=jax.ShapeDtypeStruct((n, nh, hd), F32),
        grid_spec=pltpu.PrefetchScalarGridSpec(
            num_scalar_prefetch=1,
            grid=(n // per,),
            in_specs=[pl.BlockSpec((None, None, page, nh, hd), page_map(e)) for e in range(2 * per)],
            out_specs=pl.BlockSpec((per, nh, hd), lambda s, pt: (s, 0, 0))),
        compiler_params=_cparams("arbitrary"),
        name="block_key_means",
    )(page_table_flat, *([cache_k] * (2 * per)))


def _select_kernel(km_ref, q_ref, o_ref, *, n_blocks, topk):
    q = q_ref[0]
    g = [jnp.sum(km_ref[0, n] * q, axis=1, keepdims=True) for n in range(n_blocks)]
    for s in range(topk):
        mx = g[0]
        for n in range(1, n_blocks):
            mx = jnp.maximum(mx, g[n])
        pick = jnp.full(mx.shape, n_blocks, jnp.int32)
        for n in range(n_blocks - 1, -1, -1):
            pick = jnp.where(g[n] == mx, n, pick)
        o_ref[0, s] = pick
        g = [jnp.where(pick == n, -jnp.inf, g[n]) for n in range(n_blocks)]


def _select_blocks(kmeans, q, n_seq, n_blocks, nh, topk):
    hd = HD_ATT
    kern = functools.partial(_select_kernel, n_blocks=n_blocks, topk=topk)
    return pl.pallas_call(
        kern,
        out_shape=jax.ShapeDtypeStruct((n_seq, topk, nh, 1), jnp.int32),
        grid=(n_seq,),
        in_specs=[pl.BlockSpec((1, n_blocks, nh, hd), lambda b: (b, 0, 0, 0)),
                  pl.BlockSpec((1, nh, hd), lambda b: (b, 0, 0))],
        out_specs=pl.BlockSpec((1, topk, nh, 1), lambda b: (b, 0, 0, 0)),
        compiler_params=_cparams("arbitrary"),
        name="moba_select",
    )(kmeans.reshape(n_seq, n_blocks, nh, hd), q.reshape(n_seq, nh, hd))


def _moba_sample_kernel(pt_ref, idx_ref, q_ref, kn_ref, vn_ref, ck_hbm, cv_hbm, o_ref, kbuf, vbuf, sem,
                        *, layer, scale, n_slots, n_pages, nh, page, n_seq):
    b = pl.program_id(0)
    n_pg = 2 * n_slots

    def copies(seq, slot):
        out = []
        for h in range(nh):
            for s in range(n_slots):
                blk = idx_ref[(seq * n_slots + s) * nh + h]
                for half in range(2):
                    pid = pt_ref[seq * n_pages + 2 * blk + half]
                    j = s * 2 + half
                    out.append(pltpu.make_async_copy(ck_hbm.at[layer, pid, :, h, :], kbuf.at[slot, h, j],
                                                     sem.at[0, slot]))
                    out.append(pltpu.make_async_copy(cv_hbm.at[layer, pid, :, h, :], vbuf.at[slot, h, j],
                                                     sem.at[1, slot]))
        return out

    @pl.when(b == 0)
    def _():
        for cp in copies(0, 0):
            cp.start()

    slot = b % 2

    @pl.when(b + 1 < n_seq)
    def _():
        for cp in copies(b + 1, 1 - slot):
            cp.start()

    for cp in copies(b, slot):
        cp.wait()

    for h in range(nh):
        q = q_ref[0, :, h * HD_ATT:(h + 1) * HD_ATT]
        kn = kn_ref[0, :, h * HD_ATT:(h + 1) * HD_ATT]
        vn = vn_ref[0, :, h * HD_ATT:(h + 1) * HD_ATT]
        kh = kbuf[slot, h].reshape(n_pg * page, HD_ATT)
        vh = vbuf[slot, h].reshape(n_pg * page, HD_ATT)
        s0 = jnp.sum(q * kn, axis=1, keepdims=True) * scale
        s = jnp.sum(kh * q, axis=1, keepdims=True) * scale
        m = jnp.maximum(jnp.max(s, axis=0, keepdims=True), s0)
        p = jnp.exp(s - m)
        p0 = jnp.exp(s0 - m)
        den = jnp.sum(p, axis=0, keepdims=True) + p0
        num = jnp.sum(p * vh, axis=0, keepdims=True) + p0 * vn
        o_ref[0, :, h * HD_ATT:(h + 1) * HD_ATT] = (num / den).astype(BF16)


def _moba_sample(q, k_new, v_new, cache_k, cache_v, layer, page_table_flat, sel_idx_flat, n_seq, n_pages, n_slots):
    _, _, page, nh, hd = cache_k.shape
    da = q.shape[1]
    assert MOBA_BLOCK == 2 * page and hd == HD_ATT
    kern = functools.partial(_moba_sample_kernel, layer=layer, scale=HD_ATT ** -0.5, n_slots=n_slots,
                             n_pages=n_pages, nh=nh, page=page, n_seq=n_seq)
    row = pl.BlockSpec((1, 1, da), lambda b, pt, idx: (b, 0, 0))
    out = pl.pallas_call(
        kern,
        out_shape=jax.ShapeDtypeStruct((n_seq, 1, da), BF16),
        grid_spec=pltpu.PrefetchScalarGridSpec(
            num_scalar_prefetch=2,
            grid=(n_seq,),
            in_specs=[row, row, row, pl.BlockSpec(memory_space=pl.ANY), pl.BlockSpec(memory_space=pl.ANY)],
            out_specs=row,
            scratch_shapes=[pltpu.VMEM((2, nh, 2 * n_slots, page, hd), F32),
                            pltpu.VMEM((2, nh, 2 * n_slots, page, hd), F32),
                            pltpu.SemaphoreType.DMA((2, 2))]),
        compiler_params=_cparams("arbitrary"),
        name="moba_sample",
    )(page_table_flat, sel_idx_flat, q.reshape(n_seq, 1, da), k_new.reshape(n_seq, 1, da),
      v_new.reshape(n_seq, 1, da), cache_k, cache_v)
    return out.reshape(n_seq, da)


def _shift_rows(x, tail_ref, first_ref, seq):
    tm = x.shape[0]
    local = lax.broadcasted_iota(jnp.int32, (tm, 1), 0)
    seq_off = lax.rem(pl.program_id(0) * tm, seq)
    prev = jnp.where(local == 0, tail_ref[7:8, :], pltpu.roll(x, 1, 0))
    return jnp.where((local + seq_off) == 0, first_ref[0], prev)


def _lora1_kernel(h_ref, hp_ref, *rest, seq):
    if seq is None:
        mu_ref, w1_ref, a1_ref, g1_ref, tw_ref, ta_ref, tg_ref = rest
        h = h_ref[...]
        hp = hp_ref[...]
    else:
        first_ref, mu_ref, w1_ref, a1_ref, g1_ref, tw_ref, ta_ref, tg_ref = rest
        h = h_ref[...]
        hp = _shift_rows(h, hp_ref, first_ref, seq)
    dx = hp - h
    mu = mu_ref[...]
    xw = (h + dx * mu[0:1, :]).astype(BF16)
    xa = (h + dx * mu[1:2, :]).astype(BF16)
    xg = (h + dx * mu[2:3, :]).astype(BF16)
    tw_ref[...] = jnp.tanh(_dot(xw, w1_ref[...])).astype(BF16)
    ta_ref[...] = _dot(xa, a1_ref[...]).astype(BF16)
    tg_ref[...] = jax.nn.sigmoid(_dot(xg, g1_ref[...])).astype(BF16)


def _prev_specs(tm, width, seq):
    tail = pl.BlockSpec((8, width), lambda i: (jnp.maximum(i * (tm // 8) - 1, 0), 0))
    first = pl.BlockSpec((1, 1, width), lambda i: (i * tm // seq, 0, 0))
    return tail, first


def _lora1(h, hprev, mu_wag, w1, a1, g1, seq=None):
    m, d = h.shape
    tm = _tile(m if seq is None else seq, 256, 16)
    nw, na, ng = w1.shape[1], a1.shape[1], g1.shape[1]
    row = lambda i: (i, 0)
    full = lambda i: (0, 0)
    if seq is None:
        prev_specs, prev_args = [pl.BlockSpec((tm, d), row)], [hprev]
    else:
        prev_specs, prev_args = list(_prev_specs(tm, d, seq)), [h, hprev]
    return pl.pallas_call(
        functools.partial(_lora1_kernel, seq=seq),
        out_shape=[jax.ShapeDtypeStruct((m, nw), BF16), jax.ShapeDtypeStruct((m, na), BF16),
                   jax.ShapeDtypeStruct((m, ng), BF16)],
        grid=(m // tm,),
        in_specs=[pl.BlockSpec((tm, d), row)] + prev_specs + [pl.BlockSpec(mu_wag.shape, full),
                  pl.BlockSpec((d, nw), full), pl.BlockSpec((d, na), full), pl.BlockSpec((d, ng), full)],
        out_specs=[pl.BlockSpec((tm, nw), row), pl.BlockSpec((tm, na), row), pl.BlockSpec((tm, ng), row)],
        compiler_params=_cparams("arbitrary"),
        name="rwkv_lora1",
    )(h, *prev_args, mu_wag, w1, a1, g1)


def _head_sums(x, bd):
    hi = x.astype(BF16)
    r1 = x - hi.astype(F32)
    mid = r1.astype(BF16)
    lo = (r1 - mid.astype(F32)).astype(BF16)
    return _dot(hi, bd) + _dot(mid, bd) + _dot(lo, bd)


def _block_diag_ones():
    a = lax.broadcasted_iota(jnp.int32, (LANES, LANES), 0) // HD_RWKV
    b = lax.broadcasted_iota(jnp.int32, (LANES, LANES), 1) // HD_RWKV
    return jnp.where(a == b, 1.0, 0.0).astype(BF16)


def _rwkv_prep_kernel(p_ref, pp_ref, *rest, seq):
    if seq is None:
        first_ref = None
        (tw_ref, ta_ref, tg_ref, w2_ref, a2_ref, g2_ref, mu_ref, vec_ref,
         r_ref, dec_ref, km_ref, v_ref, kk_ref, be_ref, g_ref, bo_ref) = rest
    else:
        (first_ref, tw_ref, ta_ref, tg_ref, w2_ref, a2_ref, g2_ref, mu_ref, vec_ref,
         r_ref, dec_ref, km_ref, v_ref, kk_ref, be_ref, g_ref, bo_ref) = rest
    dr = r_ref.shape[1]
    bd = _block_diag_ones()
    mu = mu_ref[...]
    vec = vec_ref[...]
    w0, a0, k_k, k_a, r_k = vec[0:1, :], vec[1:2, :], vec[2:3, :], vec[3:4, :], vec[4:5, :]
    p_all = p_ref[...]
    pp_all = pp_ref[...] if seq is None else _shift_rows(p_all, pp_ref, first_ref, seq)

    def shifted(c):
        pc = p_all[:, c * dr:(c + 1) * dr]
        return pc + (pp_all[:, c * dr:(c + 1) * dr] - pc) * mu[c:c + 1, :]

    r = shifted(0)
    k = shifted(1)
    v = shifted(2)
    w = -jnp.logaddexp(-(w0 + _dot(tw_ref[...], w2_ref[...])), 0.0) - 0.5
    dec = -jnp.exp(w)
    a = jax.nn.sigmoid(a0 + _dot(ta_ref[...], a2_ref[...]))
    g = _dot(tg_ref[...], g2_ref[...])
    kk = k * k_k
    kmod = k * (1.0 + (a - 1.0) * k_a)
    rk = r * kmod * r_k
    for c in range(dr // LANES):
        sl = slice(c * LANES, (c + 1) * LANES)
        kkc = kk[:, sl]
        kkc = kkc * lax.rsqrt(jnp.maximum(_head_sums(kkc * kkc, bd), 1e-24))
        kk_ref[:, sl] = kkc
        be_ref[:, sl] = kkc * a[:, sl]
        bo_ref[:, sl] = _head_sums(rk[:, sl], bd) * v[:, sl]
    r_ref[...] = r
    dec_ref[...] = dec
    km_ref[...] = kmod
    v_ref[...] = v
    g_ref[...] = g


def _rwkv_prep(p, pprev, tw, ta, tg, w2, a2, g2, mu_rkv, vecs, seq=None):
    m = p.shape[0]
    dr = p.shape[1] // 3
    tm = _tile(m if seq is None else seq, 128, 16)
    row = lambda i: (i, 0)
    full = lambda i: (0, 0)
    outs = [jax.ShapeDtypeStruct((m, dr), F32)] * 8
    if seq is None:
        prev_specs, prev_args = [pl.BlockSpec((tm, 3 * dr), row)], [pprev]
    else:
        prev_specs, prev_args = list(_prev_specs(tm, 3 * dr, seq)), [p, pprev]
    return pl.pallas_call(
        functools.partial(_rwkv_prep_kernel, seq=seq),
        out_shape=outs,
        grid=(m // tm,),
        in_specs=[pl.BlockSpec((tm, 3 * dr), row)] + prev_specs + [
                  pl.BlockSpec((tm, tw.shape[1]), row), pl.BlockSpec((tm, ta.shape[1]), row),
                  pl.BlockSpec((tm, tg.shape[1]), row),
                  pl.BlockSpec(w2.shape, full), pl.BlockSpec(a2.shape, full), pl.BlockSpec(g2.shape, full),
                  pl.BlockSpec(mu_rkv.shape, full), pl.BlockSpec(vecs.shape, full)],
        out_specs=[pl.BlockSpec((tm, dr), row)] * 8,
        compiler_params=_cparams("arbitrary"),
        name="rwkv_prep",
    )(p, *prev_args, tw, ta, tg, w2, a2, g2, mu_rkv, vecs)


SCAN_STEPS = 8


def _segment_ones(width):
    a = lax.broadcasted_iota(jnp.int32, (width, width), 0) // HD_RWKV
    b = lax.broadcasted_iota(jnp.int32, (width, width), 1) // HD_RWKV
    return jnp.where(a == b, 1.0, 0.0).astype(BF16)


def _rwkv_scan_kernel(kk_ref, dec_ref, be_ref, km_ref, r_ref, v_ref, s0_ref, y_ref, s_ref, *, tc, npair):
    c = pl.program_id(1)
    n = HD_RWKV
    gs = min(SCAN_STEPS, tc)
    wide = 2 if npair % 2 == 0 else 1
    assert tc % gs == 0

    @pl.when(c == 0)
    def _():
        s_ref[...] = s0_ref[...]

    seg = _segment_ones(wide * LANES)
    lane = lax.broadcasted_iota(jnp.int32, (n, LANES), 1)
    sub = lax.broadcasted_iota(jnp.int32, (n, LANES), 0)
    diag = (lane % n) == sub

    def stack(parts):
        rows = [jnp.concatenate(parts[i:i + wide], axis=1) for i in range(0, npair, wide)]
        return jnp.concatenate(rows, axis=0).astype(BF16)

    def part(full, p):
        return full[(p // wide) * n:(p // wide + 1) * n, (p % wide) * LANES:(p % wide + 1) * LANES]

    def steps(t0):
        def row(ref, p, r):
            return ref[pl.ds(t0, gs), p * LANES:(p + 1) * LANES][r:r + 1, :]

        y_rows = [[None] * gs for _ in range(npair)]

        def finish(y_all, r):
            for p in range(npair):
                y_rows[p][r] = jnp.sum(jnp.where(diag, part(y_all, p), 0.0), axis=0, keepdims=True)

        pending = None
        for r in range(gs):
            sa_all = _dot(stack([s_ref[0, p] * row(kk_ref, p, r) for p in range(npair)]), seg)
            vb_all = _dot(stack([jnp.where(diag, row(v_ref, p, r), 0.0) for p in range(npair)]), seg)
            if pending is not None:
                finish(pending, r - 1)
            ys = []
            for p in range(npair):
                s_new = (s_ref[0, p] * jnp.exp(row(dec_ref, p, r)) - part(sa_all, p) * row(be_ref, p, r)
                         + part(vb_all, p) * row(km_ref, p, r))
                s_ref[0, p] = s_new
                ys.append(s_new * row(r_ref, p, r))
            pending = _dot(stack(ys), seg)
        finish(pending, gs - 1)
        for p in range(npair):
            y_ref[pl.ds(t0, gs), p * LANES:(p + 1) * LANES] = jnp.concatenate(y_rows[p], axis=0)

    if tc == gs:
        steps(0)
    else:
        def body(i, carry):
            steps(pl.multiple_of(i * gs, gs))
            return carry

        lax.fori_loop(0, tc // gs, body, 0)


def _rwkv_scan(kk, dec, be, km, r, v, s0, batch, seq):
    m, dr = kk.shape
    npair = dr // LANES
    tc = _tile(seq, 128, 8)
    nc = seq // tc
    if seq == 1:
        kk, dec, be, km, r, v = (a.reshape(m, 1, dr) for a in (kk, dec, be, km, r, v))
        op = pl.BlockSpec((None, 1, dr), lambda b, c: (b, 0, 0))
        y_shape = jax.ShapeDtypeStruct((m, 1, dr), F32)
    else:
        op = pl.BlockSpec((tc, dr), lambda b, c: (b * nc + c, 0))
        y_shape = jax.ShapeDtypeStruct((m, dr), F32)
    kern = functools.partial(_rwkv_scan_kernel, tc=tc, npair=npair)
    st = pl.BlockSpec((1, npair, HD_RWKV, LANES), lambda b, c: (b, 0, 0, 0))
    y, s_fin = pl.pallas_call(
        kern,
        out_shape=[y_shape, jax.ShapeDtypeStruct((batch, npair, HD_RWKV, LANES), F32)],
        grid=(batch, nc),
        in_specs=[op] * 6 + [st],
        out_specs=[op, st],
        compiler_params=_cparams("arbitrary", "arbitrary"),
        name="rwkv_scan",
    )(kk, dec, be, km, r, v, s0)
    return y.reshape(m, dr), s_fin


CHUNK = 64


def _rwkv_chunk_kernel(kk_ref, ld_ref, be_ref, km_ref, r_ref, v_ref, s0_ref, y_ref, s_ref, *, npair):
    c = pl.program_id(1)
    L = CHUNK
    n = HD_RWKV

    @pl.when(c == 0)
    def _():
        s_ref[...] = s0_ref[...]

    lane = lax.broadcasted_iota(jnp.int32, (L, LANES), 1)
    trow = lax.broadcasted_iota(jnp.int32, (L, LANES), 0)
    head0 = lane < n
    src = lane % n
    strict = trow > src
    incl = trow >= src
    tri = (lax.broadcasted_iota(jnp.int32, (L, L), 0) >= lax.broadcasted_iota(jnp.int32, (L, L), 1))
    tri = jnp.where(tri, 1.0, 0.0).astype(BF16)
    bd_r = lax.broadcasted_iota(jnp.int32, (LANES, LANES), 0) // n
    bd_c = lax.broadcasted_iota(jnp.int32, (LANES, LANES), 1) // n
    bdmask = bd_r == bd_c

    def blk(x):
        return jnp.concatenate([jnp.where(head0, x, 0.0), jnp.where(head0, 0.0, x)], axis=0).astype(BF16)

    def bf(x):
        return x.astype(BF16)

    pairs = range(npair)
    sls = [slice(p * LANES, (p + 1) * LANES) for p in pairs]

    cum = []
    for p in pairs:
        ld = ld_ref[:, sls[p]]
        hi = bf(ld)
        r1 = ld - hi.astype(F32)
        mid = bf(r1)
        lo = bf(r1 - mid.astype(F32))
        cum.append(_dot(tri, hi) + _dot(tri, mid) + _dot(tri, lo))

    z, s0t = [], []
    for p in pairs:
        e_neg = jnp.exp(-cum[p])
        abar = -kk_ref[:, sls[p]] * jnp.exp(cum[p] - ld_ref[:, sls[p]])
        rbar = r_ref[:, sls[p]] * jnp.exp(cum[p])
        lhs = bf(jnp.concatenate([abar, rbar], axis=0))
        rhs = jnp.concatenate([blk(be_ref[:, sls[p]] * e_neg), blk(km_ref[:, sls[p]] * e_neg)], axis=0)
        z.append(_dot_nt(lhs, rhs))
        s0t.append(_dot_nt(lhs, bf(s_ref[0, p])))

    u, pw = [], []
    for p in pairs:
        amat = jnp.where(strict, z[p][:L, 2 * L:], 0.0)
        u.append(s0t[p][:L] + _dot(bf(amat), blk(v_ref[:, sls[p]])))
        pw.append(jnp.where(strict, z[p][:L, :2 * L], 0.0))

    n_fac = L.bit_length() - 1
    for it in range(n_fac):
        u = [u[p] + _dot(bf(pw[p]), blk(u[p])) for p in pairs]
        if it + 1 < n_fac:
            pw = [_dot(bf(pw[p]), blk(pw[p])) for p in pairs]

    for p in pairs:
        rbk = jnp.concatenate([jnp.where(incl, z[p][L:, :2 * L], 0.0), jnp.where(incl, z[p][L:, 2 * L:], 0.0)],
                              axis=1)
        uv = jnp.concatenate([blk(u[p]), blk(v_ref[:, sls[p]])], axis=0)
        y_ref[:, sls[p]] = s0t[p][L:] + _dot(bf(rbk), uv)

    for p in pairs:
        cl = cum[p][L - 1:L, :]
        e_end = jnp.exp(cl - cum[p])
        uv_t = jnp.transpose(jnp.concatenate([u[p], v_ref[:, sls[p]]], axis=0))
        ends = jnp.concatenate([be_ref[:, sls[p]] * e_end, km_ref[:, sls[p]] * e_end], axis=0)
        cmat = _dot(bf(uv_t), bf(ends))
        s_ref[0, p] = s_ref[0, p] * jnp.exp(cl) + jnp.where(bdmask, cmat, 0.0)


def _rwkv_chunked(kk, ld, be, km, r, v, s0, batch, seq):
    m, dr = kk.shape
    npair = dr // LANES
    assert seq % CHUNK == 0 and 2 * CHUNK == LANES
    nc = seq // CHUNK
    op = pl.BlockSpec((CHUNK, dr), lambda b, c: (b * nc + c, 0))
    st = pl.BlockSpec((1, npair, LANES, LANES), lambda b, c: (b, 0, 0, 0))
    return pl.pallas_call(
        functools.partial(_rwkv_chunk_kernel, npair=npair),
        out_shape=[jax.ShapeDtypeStruct((m, dr), F32), jax.ShapeDtypeStruct((batch, npair, LANES, LANES), F32)],
        grid=(batch, nc),
        in_specs=[op] * 6 + [st],
        out_specs=[op, st],
        compiler_params=_cparams("arbitrary", "arbitrary"),
        name="rwkv_chunked",
    )(kk, ld, be, km, r, v, s0)


def _rwkv_post_kernel(y_ref, bo_ref, g_ref, ln_ref, o_ref):
    bd = _block_diag_ones()
    ln = ln_ref[...]
    inv_n = 1.0 / HD_RWKV
    for c in range(y_ref.shape[1] // LANES):
        sl = slice(c * LANES, (c + 1) * LANES)
        y = y_ref[:, sl]
        mean = _head_sums(y, bd) * inv_n
        d = y - mean
        var = _head_sums(d * d, bd) * inv_n
        yn = d * lax.rsqrt(var + GN_EPS) * ln[0:1, sl] + ln[1:2, sl]
        o_ref[:, sl] = ((yn + bo_ref[:, sl]) * g_ref[:, sl]).astype(BF16)


def _rwkv_post(y, bonus, g, ln):
    m, dr = y.shape
    tm = _tile(m, 256, 16)
    row = lambda i: (i, 0)
    return pl.pallas_call(
        _rwkv_post_kernel,
        out_shape=jax.ShapeDtypeStruct((m, dr), BF16),
        grid=(m // tm,),
        in_specs=[pl.BlockSpec((tm, dr), row)] * 3 + [pl.BlockSpec(ln.shape, lambda i: (0, 0))],
        out_specs=pl.BlockSpec((tm, dr), row),
        compiler_params=_cparams("arbitrary"),
        name="rwkv_post",
    )(y, bonus, g, ln)


def _pack_state(s):
    b, h, n, _ = s.shape
    return s.reshape(b, h // 2, 2, n, n).transpose(0, 1, 3, 2, 4).reshape(b, h // 2, n, 2 * n)


def _unpack_state(s):
    b, p, n, _ = s.shape
    return s.reshape(b, p, n, 2, n).transpose(0, 1, 3, 2, 4).reshape(b, 2 * p, n, n)


def _blockdiag_state(s):
    b, h, n, _ = s.shape
    s = s.reshape(b, h // 2, 2, n, n)
    z = jnp.zeros_like(s[:, :, 0])
    top = jnp.concatenate([s[:, :, 0], z], axis=-1)
    bot = jnp.concatenate([z, s[:, :, 1]], axis=-1)
    return jnp.concatenate([top, bot], axis=-2)


def _blockdiag_to_heads(s):
    b, p, _, _ = s.shape
    n = HD_RWKV
    return jnp.stack([s[:, :, :n, :n], s[:, :, n:, n:]], axis=2).reshape(b, 2 * p, n, n)


def _pad_cols(w, n):
    return jnp.pad(w, ((0, 0), (0, n - w.shape[1])))


def _pad_rows(w, n):
    return jnp.pad(w, ((0, n - w.shape[0]), (0, 0)))


def _rope_tables(pos, rows_per_pos):
    half = HD_ATT // 2
    inv = ROPE_THETA ** (-jnp.arange(half, dtype=F32) / half)
    ang = pos.astype(F32)[:, None] * inv[None, :]
    cos = jnp.cos(ang)
    sin = jnp.sin(ang)
    cos2 = jnp.concatenate([cos, cos], axis=-1)
    sin2 = jnp.concatenate([-sin, sin], axis=-1)
    if rows_per_pos is not None:
        cos2 = jnp.broadcast_to(cos2, (rows_per_pos, HD_ATT))
        sin2 = jnp.broadcast_to(sin2, (rows_per_pos, HD_ATT))
    return cos2, sin2


def kernel(x_prompt, x_sample, cache_k, cache_v, state_shift, state_wkv, page_table, c_prompt, c_sample, g_norm, w_ada, b_ada, ffn_w_gate, ffn_w_up, ffn_w_down, w_in, w_out, mu_rkv, mu_wag, w0, w_lora1, w_lora2, a0, a_lora1, a_lora2, g_lora1, g_lora2, k_k, k_a, r_k, ln_x_w, ln_x_b, g_final):
    B, T, D = x_prompt.shape
    DB, DS, _ = x_sample.shape
    depth = g_norm.shape[0]
    assert DS == 1
    n_pages = page_table.shape[1]
    page = cache_k.shape[2]
    past_len = n_pages * page
    assert past_len % MOBA_BLOCK == 0 and MOBA_BLOCK == 2 * page
    da = cache_k.shape[3] * cache_k.shape[4]
    nh = da // HD_ATT
    dr = D - da
    n_slots = min(MOBA_TOPK, past_len // MOBA_BLOCK)

    xp = x_prompt.reshape(B * T, D)
    xs = x_sample.reshape(DB, D)
    pt_flat = page_table.reshape(-1).astype(jnp.int32)
    cos_p, sin_p = _rope_tables(jnp.arange(T, dtype=jnp.int32), None)
    cos_p = jnp.tile(cos_p, (B, 1))
    sin_p = jnp.tile(sin_p, (B, 1))

    outs = [[] for _ in range(8)]
    n_c = B + DB
    n_cp = -(-n_c // 8) * 8
    c_all = jnp.pad(jnp.concatenate([c_prompt, c_sample], axis=0), ((0, n_cp - n_c), (0, 0)))

    for l in range(depth):
        bf = lambda w: w.astype(BF16)
        wout = bf(w_out[l])
        lw = -(-w_lora1.shape[2] // LANES) * LANES
        la = -(-a_lora1.shape[2] // LANES) * LANES
        lg = -(-g_lora1.shape[2] // LANES) * LANES
        w1, a1, g1 = bf(_pad_cols(w_lora1[l], lw)), bf(_pad_cols(a_lora1[l], la)), bf(_pad_cols(g_lora1[l], lg))
        w2, a2, g2 = bf(_pad_rows(w_lora2[l], lw)), bf(_pad_rows(a_lora2[l], la)), bf(_pad_rows(g_lora2[l], lg))
        vecs = jnp.stack([w0[l], a0[l], k_k[l], k_a[l], r_k[l].reshape(dr)], axis=0)
        vecs = jnp.pad(vecs, ((0, 3), (0, 0)))
        ln = jnp.pad(jnp.stack([ln_x_w[l], ln_x_b[l]], axis=0), ((0, 6), (0, 0)))
        mu3 = jnp.pad(mu_rkv[l], ((0, 5), (0, 0)))
        muw = jnp.pad(mu_wag[l], ((0, 5), (0, 0)))

        mod = _ada(c_all, w_ada[l], b_ada[l])
        chunks = [mod[:, k * D:(k + 1) * D] for k in range(9)]
        mod_p = [ch[:B].reshape(B, 1, D) for ch in chunks]
        mod_s = [ch[B:B + DB].reshape(1, DB, D) for ch in chunks]

        def ffn(xp_, xs_, idx):
            sub = 2 * idx
            (hp_,) = _modulate(xp_, g_norm[l, sub], mod_p[3 * sub], mod_p[3 * sub + 1], T, False)
            (hs_,) = _modulate(xs_, g_norm[l, sub], mod_s[3 * sub], mod_s[3 * sub + 1], 1, False)
            u_p, wd_b, u_s = _ffn_up(hp_, ffn_w_gate, ffn_w_up, l, idx, ffn_w_down, rider=hs_)
            return _ffn_down(u_p, wd_b, xp_, mod_p[3 * sub + 2], T,
                             rider=(u_s, xs_, mod_s[3 * sub + 2].reshape(DB, D)))

        xp, xs = ffn(xp, xs, 0)

        hb, hf = _modulate(xp, g_norm[l, 1], mod_p[3], mod_p[4], T, True)
        hbs, hfs = _modulate(xs, g_norm[l, 1], mod_s[3], mod_s[4], 1, True)
        n_rid = 2 * DB + B
        n_rid_p = -(-n_rid // 16) * 16
        riders = jnp.concatenate([hbs, bf(state_shift[l]), jnp.zeros((n_rid_p - 2 * DB, D), BF16)], axis=0)
        cos_r, sin_r = _rope_tables(jnp.full((1,), past_len, dtype=jnp.int32), n_rid_p)
        q, q_r = _proj(hb, w_in, l, 0, da, (cos_p, sin_p), riders, (cos_r, sin_r))
        k, k_r = _proj(hb, w_in, l, da, da, (cos_p, sin_p), riders, (cos_r, sin_r))
        v, v_r = _proj(hb, w_in, l, 2 * da, da, None, riders)
        prk, p_r = _proj(hb, w_in, l, 3 * da, 3 * dr, None, riders)
        qs, ks, vs, prk_s = q_r[:DB], k_r[:DB], v_r[:DB], p_r[:DB]
        p_first = p_r[DB:2 * DB + B]
        att = _moba_prompt(q, k, v, B, T)

        h3 = hf.reshape(B, T, D)

        def rwkv(hf_, hprev_, prk_, pprev_, s0, nb_, nt_):
            seq = None if nt_ == 1 else nt_
            tw, ta, tg = _lora1(hf_, hprev_, muw, w1, a1, g1, seq)
            r_, dec_, km_, v_, kk_, be_, g_, bo_ = _rwkv_prep(prk_, pprev_, tw, ta, tg, w2, a2, g2, mu3, vecs, seq)
            if nt_ % CHUNK == 0:
                y, s_fin = _rwkv_chunked(kk_, dec_, be_, km_, r_, v_, _blockdiag_state(s0), nb_, nt_)
                s_fin = _blockdiag_to_heads(s_fin)
            else:
                y, s_fin = _rwkv_scan(kk_, dec_, be_, km_, r_, v_, _pack_state(s0), nb_, nt_)
                s_fin = _unpack_state(s_fin)
            rw = _rwkv_post(y, bo_, g_, ln)
            return rw, s_fin

        s0_p = jnp.zeros((B, dr // HD_RWKV, HD_RWKV, HD_RWKV), F32)
        rw_p, sfin_p = rwkv(hf, jnp.zeros((B, 1, D), F32), prk, p_first[DB:DB + B].reshape(B, 1, 3 * dr),
                            s0_p, B, T)
        xp = _out_proj(att, rw_p, wout, xp, mod_p[5], T)

        n_blocks = past_len // MOBA_BLOCK
        kmeans = _block_means(cache_k, l, pt_flat)
        sel = _select_blocks(kmeans, qs, DB, n_blocks, nh, n_slots)
        att_s = _moba_sample(qs, ks, vs, cache_k, cache_v, l, pt_flat, sel.reshape(-1), DB, n_pages, n_slots)

        rw_s, sfin_s = rwkv(hfs, state_shift[l], prk_s, p_first[:DB], state_wkv[l], DB, 1)
        xs = _out_proj(att_s, rw_s, wout, xs, mod_s[5], 1)
        xp, xs = ffn(xp, xs, 1)

        outs[0].append(k.reshape(B, T, nh, HD_ATT))
        outs[1].append(v.reshape(B, T, nh, HD_ATT))
        outs[2].append(h3[:, -1])
        outs[3].append(sfin_p)
        outs[4].append(ks.reshape(DB, 1, nh, HD_ATT))
        outs[5].append(vs.reshape(DB, 1, nh, HD_ATT))
        outs[6].append(hfs)
        outs[7].append(sfin_s)

    y_prompt = _rmsnorm(xp, g_final).reshape(B, T, D)
    y_sample = _rmsnorm(xs, g_final).reshape(DB, 1, D)
    return (y_prompt, y_sample) + tuple(jnp.stack(o) for o in outs)
```

```python
import functools

import jax
import jax.numpy as jnp
from jax import lax
from jax.experimental import pallas as pl
from jax.experimental.pallas import tpu as pltpu

F32 = jnp.float32
BF16 = jnp.bfloat16

HD_ATT = 128
HD_RWKV = 64
LANES = 128
MOBA_BLOCK = 256
MOBA_TOPK = 3
ROPE_THETA = 10000.0
RMS_EPS = 1e-6
GN_EPS = 64e-5
NEG_BIG = -1e30
VMEM_LIMIT = 56 * 1024 * 1024


def _cparams(*sem):
    return pltpu.CompilerParams(dimension_semantics=sem, vmem_limit_bytes=VMEM_LIMIT)


def _tile(n, pref, align):
    if n <= pref:
        return n
    t = (pref // align) * align
    while t >= align:
        if n % t == 0:
            return t
        t -= align
    return n


def _dot(a, b):
    return jnp.dot(a, b, preferred_element_type=F32)


def _dot_nt(a, b):
    return lax.dot_general(a, b, (((1,), (1,)), ((), ())), preferred_element_type=F32)


def _ada_kernel(c_ref, w_ref, b_ref, o_ref):
    c = c_ref[...]
    s = (c * jax.nn.sigmoid(c)).astype(BF16)
    o_ref[...] = _dot(s, w_ref[...].astype(BF16)) + b_ref[...]


def _ada(c, w, b):
    m, d = c.shape
    n = w.shape[1]
    tn = _tile(n, 512, LANES)
    return pl.pallas_call(
        _ada_kernel,
        out_shape=jax.ShapeDtypeStruct((m, n), F32),
        grid=(n // tn,),
        in_specs=[pl.BlockSpec((m, d), lambda j: (0, 0)),
                  pl.BlockSpec((d, tn), lambda j: (0, j)),
                  pl.BlockSpec((1, tn), lambda j: (0, j))],
        out_specs=pl.BlockSpec((m, tn), lambda j: (0, j)),
        compiler_params=_cparams("arbitrary"),
        name="ada_proj",
    )(c, w, b.reshape(1, n))


def _modulate_kernel(x_ref, g_ref, sh_ref, sc_ref, *o_refs):
    x = x_ref[...]
    y = x * lax.rsqrt(jnp.mean(x * x, axis=-1, keepdims=True) + RMS_EPS) * g_ref[...]
    h = y * (1.0 + sc_ref[0]) + sh_ref[0]
    o_refs[0][...] = h.astype(BF16)
    if len(o_refs) > 1:
        o_refs[1][...] = h


def _modulate(x, g, sh, sc, rows_per_group, want_f32):
    m, d = x.shape
    r = sh.shape[1]
    tm = m if r > 1 else _tile(rows_per_group, 256, 16)
    grp = (lambda i: (0, 0, 0)) if r > 1 else (lambda i: (i * tm // rows_per_group, 0, 0))
    out_shape = [jax.ShapeDtypeStruct((m, d), BF16)]
    out_specs = [pl.BlockSpec((tm, d), lambda i: (i, 0))]
    if want_f32:
        out_shape.append(jax.ShapeDtypeStruct((m, d), F32))
        out_specs.append(pl.BlockSpec((tm, d), lambda i: (i, 0)))
    return pl.pallas_call(
        _modulate_kernel,
        out_shape=out_shape,
        grid=(m // tm,),
        in_specs=[pl.BlockSpec((tm, d), lambda i: (i, 0)),
                  pl.BlockSpec((1, d), lambda i: (0, 0)),
                  pl.BlockSpec((1, r, d), grp),
                  pl.BlockSpec((1, r, d), grp)],
        out_specs=out_specs,
        compiler_params=_cparams("arbitrary"),
        name="modulate",
    )(x, g.reshape(1, d), sh, sc)


def _rmsnorm_kernel(x_ref, g_ref, o_ref):
    x = x_ref[...]
    o_ref[...] = x * lax.rsqrt(jnp.mean(x * x, axis=-1, keepdims=True) + RMS_EPS) * g_ref[...]


def _rmsnorm(x, g):
    m, d = x.shape
    tm = _tile(m, 256, 8)
    return pl.pallas_call(
        _rmsnorm_kernel,
        out_shape=jax.ShapeDtypeStruct((m, d), F32),
        grid=(m // tm,),
        in_specs=[pl.BlockSpec((tm, d), lambda i: (i, 0)), pl.BlockSpec((1, d), lambda i: (0, 0))],
        out_specs=pl.BlockSpec((tm, d), lambda i: (i, 0)),
        compiler_params=_cparams("arbitrary"),
        name="final_rmsnorm",
    )(x, g.reshape(1, d))


def _first_pass(nj):
    return lambda i, j: jnp.where(i == 0, j, nj - 1)


def _ffn_up_kernel(h_ref, wg_ref, wu_ref, *rest, has_wd, has_rider):
    rest = list(rest)
    wd_ref = rest.pop(0) if has_wd else None
    hs_ref = rest.pop(0) if has_rider else None
    o_ref = rest.pop(0)
    wdb_ref = rest.pop(0) if has_wd else None
    us_ref = rest.pop(0) if has_rider else None
    wg = wg_ref[...].astype(BF16)
    wu = wu_ref[...].astype(BF16)

    def gated(h):
        g = _dot(h, wg)
        return (g * jax.nn.sigmoid(g) * _dot(h, wu)).astype(BF16)

    if has_wd or has_rider:
        @pl.when(pl.program_id(0) == 0)
        def _():
            if has_wd:
                wdb_ref[...] = wd_ref[...].astype(BF16)
            if has_rider:
                us_ref[...] = gated(hs_ref[...])

    o_ref[...] = gated(h_ref[...])


def _ffn_up(h, wg_all, wu_all, layer, idx, wd_all=None, rider=None):
    m, d = h.shape
    f = wg_all.shape[-1]
    tm = _tile(m, 1024, 16)
    tn = _tile(f, 256, LANES)
    nj = f // tn
    fp = _first_pass(nj)
    w_spec = pl.BlockSpec((None, None, d, tn), lambda i, j: (layer, idx, 0, j))
    in_specs = [pl.BlockSpec((tm, d), lambda i, j: (i, 0)), w_spec, w_spec]
    out_shape = [jax.ShapeDtypeStruct((m, f), BF16)]
    out_specs = [pl.BlockSpec((tm, tn), lambda i, j: (i, j))]
    args = [h, wg_all, wu_all]
    if wd_all is not None:
        in_specs.append(pl.BlockSpec((None, None, tn, d), lambda i, j: (layer, idx, fp(i, j), 0)))
        out_shape.append(jax.ShapeDtypeStruct((f, d), BF16))
        out_specs.append(pl.BlockSpec((tn, d), lambda i, j: (fp(i, j), 0)))
        args.append(wd_all)
    if rider is not None:
        ms = rider.shape[0]
        in_specs.append(pl.BlockSpec((ms, d), lambda i, j: (0, 0)))
        out_shape.append(jax.ShapeDtypeStruct((ms, f), BF16))
        out_specs.append(pl.BlockSpec((ms, tn), lambda i, j: (0, fp(i, j))))
        args.append(rider)
    res = pl.pallas_call(
        functools.partial(_ffn_up_kernel, has_wd=wd_all is not None, has_rider=rider is not None),
        out_shape=out_shape,
        grid=(m // tm, nj),
        in_specs=in_specs,
        out_specs=out_specs,
        compiler_params=_cparams("arbitrary", "arbitrary"),
        name="ffn_up",
    )(*args)
    return res if len(res) > 1 else res[0]


def _ffn_down_kernel(u_ref, w_ref, x_ref, ga_ref, *rest):
    w = w_ref[...]
    if len(rest) == 1:
        (o_ref,) = rest
    else:
        us_ref, xs_ref, gas_ref, o_ref, os_ref = rest

        @pl.when(pl.program_id(0) == 0)
        def _():
            os_ref[...] = xs_ref[...] + 0.5 * gas_ref[...] * _dot(us_ref[...], w)

    o_ref[...] = x_ref[...] + 0.5 * ga_ref[0] * _dot(u_ref[...], w)


def _ffn_down(u, wd, x, ga, rows_per_group, rider=None):
    m, f = u.shape
    d = wd.shape[1]
    r = ga.shape[1]
    tm = m if r > 1 else _tile(rows_per_group, 512, 16)
    tn = _tile(d, 256, LANES)
    nj = d // tn
    grp = (lambda i, j: (0, 0, j)) if r > 1 else (lambda i, j: (i * tm // rows_per_group, 0, j))
    in_specs = [pl.BlockSpec((tm, f), lambda i, j: (i, 0)),
                pl.BlockSpec((f, tn), lambda i, j: (0, j)),
                pl.BlockSpec((tm, tn), lambda i, j: (i, j)),
                pl.BlockSpec((1, r, tn), grp)]
    out_shape = [jax.ShapeDtypeStruct((m, d), F32)]
    out_specs = [pl.BlockSpec((tm, tn), lambda i, j: (i, j))]
    args = [u, wd, x, ga]
    if rider is not None:
        ms = rider[0].shape[0]
        fp = _first_pass(nj)
        tile_s = pl.BlockSpec((ms, tn), lambda i, j: (0, fp(i, j)))
        in_specs += [pl.BlockSpec((ms, f), lambda i, j: (0, 0)), tile_s, tile_s]
        out_shape.append(jax.ShapeDtypeStruct((ms, d), F32))
        out_specs.append(tile_s)
        args += list(rider)
    res = pl.pallas_call(
        _ffn_down_kernel,
        out_shape=out_shape,
        grid=(m // tm, nj),
        in_specs=in_specs,
        out_specs=out_specs,
        compiler_params=_cparams("arbitrary", "arbitrary"),
        name="ffn_down",
    )(*args)
    return res if rider is not None else res[0]


def _rope_store(o_ref, x, cos, sin):
    for c in range(x.shape[1] // HD_ATT):
        xc = x[:, c * HD_ATT:(c + 1) * HD_ATT]
        o_ref[:, c * HD_ATT:(c + 1) * HD_ATT] = xc * cos + pltpu.roll(xc, HD_ATT // 2, 1) * sin


def _proj_kernel(h_ref, w_ref, *rest, has_rope, has_rider):
    rest = list(rest)
    cos_ref, sin_ref = (rest.pop(0), rest.pop(0)) if has_rope else (None, None)
    hs_ref = rest.pop(0) if has_rider else None
    cs_ref, ss_ref = (rest.pop(0), rest.pop(0)) if has_rope and has_rider else (None, None)
    o_ref = rest.pop(0)
    os_ref = rest.pop(0) if has_rider else None
    w = w_ref[...].astype(BF16)

    def emit(out_ref, rows, cos_r, sin_r):
        x = _dot(rows, w)
        if has_rope:
            _rope_store(out_ref, x, cos_r[...], sin_r[...])
        else:
            out_ref[...] = x

    if has_rider:
        @pl.when(pl.program_id(0) == 0)
        def _():
            emit(os_ref, hs_ref[...], cs_ref, ss_ref)

    emit(o_ref, h_ref[...], cos_ref, sin_ref)


def _proj(h, w, layer, col0, ncols, rope=None, rider=None, rider_rope=None):
    m, d = h.shape
    tm = _tile(m, 1024, 16)
    tn = _tile(ncols, 512, LANES)
    assert col0 % tn == 0
    j0 = col0 // tn
    nj = ncols // tn
    in_specs = [pl.BlockSpec((tm, d), lambda i, j: (i, 0)),
                pl.BlockSpec((None, d, tn), lambda i, j: (layer, 0, j + j0))]
    args = [h, w]
    out_shape = [jax.ShapeDtypeStruct((m, ncols), F32)]
    out_specs = [pl.BlockSpec((tm, tn), lambda i, j: (i, j))]
    if rope is not None:
        in_specs += [pl.BlockSpec((tm, HD_ATT), lambda i, j: (i, 0))] * 2
        args += list(rope)
    if rider is not None:
        ms = rider.shape[0]
        fp = _first_pass(nj)
        in_specs.append(pl.BlockSpec((ms, d), lambda i, j: (0, 0)))
        args.append(rider)
        if rope is not None:
            in_specs += [pl.BlockSpec((ms, HD_ATT), lambda i, j: (0, 0))] * 2
            args += list(rider_rope)
        out_shape.append(jax.ShapeDtypeStruct((ms, ncols), F32))
        out_specs.append(pl.BlockSpec((ms, tn), lambda i, j: (0, fp(i, j))))
    res = pl.pallas_call(
        functools.partial(_proj_kernel, has_rope=rope is not None, has_rider=rider is not None),
        out_shape=out_shape,
        grid=(m // tm, nj),
        in_specs=in_specs,
        out_specs=out_specs,
        compiler_params=_cparams("arbitrary", "arbitrary"),
        name="in_proj_rope" if rope is not None else "in_proj",
    )(*args)
    return res if rider is not None else res[0]


def _out_proj_kernel(att_ref, rw_ref, wa_ref, wr_ref, x_ref, ga_ref, o_ref):
    y = _dot(att_ref[...], wa_ref[...]) + _dot(rw_ref[...], wr_ref[...])
    o_ref[...] = x_ref[...] + ga_ref[0] * y


def _out_proj(att, rw, w, x, ga, rows_per_group):
    m, da = att.shape
    dr = rw.shape[1]
    d = w.shape[1]
    assert da == dr
    r = ga.shape[1]
    tm = m if r > 1 else _tile(rows_per_group, 1024, 16)
    tn = _tile(d, 512, LANES)
    grp = (lambda i, j: (0, 0, j)) if r > 1 else (lambda i, j: (i * tm // rows_per_group, 0, j))
    return pl.pallas_call(
        _out_proj_kernel,
        out_shape=jax.ShapeDtypeStruct((m, d), F32),
        grid=(m // tm, d // tn),
        in_specs=[pl.BlockSpec((tm, da), lambda i, j: (i, 0)),
                  pl.BlockSpec((tm, dr), lambda i, j: (i, 0)),
                  pl.BlockSpec((da, tn), lambda i, j: (0, j)),
                  pl.BlockSpec((dr, tn), lambda i, j: (1, j)),
                  pl.BlockSpec((tm, tn), lambda i, j: (i, j)),
                  pl.BlockSpec((1, r, tn), grp)],
        out_specs=pl.BlockSpec((tm, tn), lambda i, j: (i, j)),
        compiler_params=_cparams("arbitrary", "arbitrary"),
        name="out_proj",
    )(att, rw, w, w, x, ga)


def _rank_select(g_rows, n_valid, topk):
    nb = len(g_rows)
    sel = []
    for j in range(nb):
        rank = jnp.zeros_like(g_rows[j])
        for n in range(nb):
            if n == j:
                continue
            beats = (g_rows[n] > g_rows[j]) if n > j else (g_rows[n] >= g_rows[j])
            rank = rank + jnp.where(beats, jnp.where(n < n_valid, 1.0, 0.0), 0.0)
        sel.append(rank < float(topk))
    return sel


MOBA_HEADS_PER_STEP = 8


def _moba_prompt_kernel(q_ref, k_ref, v_ref, o_ref, kb_ref, vt_ref, km_ref, m_ref, l_ref, acc_ref,
                        *, nb, scale, hp):
    i = pl.program_id(2)
    blk = MOBA_BLOCK
    heads = [slice(hh * HD_ATT, (hh + 1) * HD_ATT) for hh in range(hp)]

    @pl.when(i == 0)
    def _():
        km_ref[...] = jnp.zeros_like(km_ref)
        for hh in range(hp):
            for j in range(nb):
                kj = k_ref[j * blk:(j + 1) * blk, heads[hh]]
                km_ref[hh, j:j + 1, :] = jnp.mean(kj, axis=0, keepdims=True)
                kb_ref[hh, j] = kj.astype(BF16)
                vt_ref[hh, j] = jnp.transpose(v_ref[j * blk:(j + 1) * blk, heads[hh]]).astype(BF16)

    kpos = lax.broadcasted_iota(jnp.int32, (blk, blk), 0)
    qpos = lax.broadcasted_iota(jnp.int32, (blk, blk), 1)
    hs = range(hp)
    qf = [q_ref[:, heads[hh]] for hh in hs]
    gt = [_dot_nt(km_ref[hh].astype(BF16), qf[hh].astype(BF16)) for hh in hs]
    sel = [_rank_select([gt[hh][j:j + 1, :] for j in range(nb)], i, MOBA_TOPK) for hh in hs]
    qb = [(qf[hh] * scale).astype(BF16) for hh in hs]

    s_own = [_dot_nt(kb_ref[hh, i], qb[hh]) for hh in hs]
    p_own = []
    for hh in hs:
        s = jnp.where(kpos <= qpos, s_own[hh], NEG_BIG)
        m0 = jnp.max(s, axis=0, keepdims=True)
        p = jnp.exp(s - m0)
        m_ref[hh] = m0
        l_ref[hh] = jnp.sum(p, axis=0, keepdims=True)
        p_own.append(p.astype(BF16))
    for hh in hs:
        acc_ref[hh] = _dot(vt_ref[hh, i], p_own[hh])

    for j in range(nb - 1):
        @pl.when(j < i)
        def _(j=j):
            s_all = [_dot_nt(kb_ref[hh, j], qb[hh]) for hh in hs]
            p_all, alphas = [], []
            for hh in hs:
                s = jnp.where(sel[hh][j], s_all[hh], NEG_BIG)
                m_old = m_ref[hh]
                m_new = jnp.maximum(m_old, jnp.max(s, axis=0, keepdims=True))
                alpha = jnp.exp(m_old - m_new)
                p = jnp.exp(s - m_new)
                m_ref[hh] = m_new
                l_ref[hh] = alpha * l_ref[hh] + jnp.sum(p, axis=0, keepdims=True)
                p_all.append(p.astype(BF16))
                alphas.append(alpha)
            for hh in hs:
                acc_ref[hh] = alphas[hh] * acc_ref[hh] + _dot(vt_ref[hh, j], p_all[hh])

    for hh in hs:
        o_ref[:, heads[hh]] = jnp.transpose(acc_ref[hh] / l_ref[hh]).astype(BF16)


def _moba_prompt(q, k, v, batch, seq):
    m, da = q.shape
    nh = da // HD_ATT
    assert seq % MOBA_BLOCK == 0
    nb = seq // MOBA_BLOCK
    nbp = -(-nb // 8) * 8
    hp = _tile(nh, MOBA_HEADS_PER_STEP, 1)
    w = hp * HD_ATT
    kern = functools.partial(_moba_prompt_kernel, nb=nb, scale=HD_ATT ** -0.5, hp=hp)
    return pl.pallas_call(
        kern,
        out_shape=jax.ShapeDtypeStruct((m, da), BF16),
        grid=(batch, nh // hp, nb),
        in_specs=[pl.BlockSpec((MOBA_BLOCK, w), lambda b, h, i: (b * nb + i, h)),
                  pl.BlockSpec((seq, w), lambda b, h, i: (b, h)),
                  pl.BlockSpec((seq, w), lambda b, h, i: (b, h))],
        out_specs=pl.BlockSpec((MOBA_BLOCK, w), lambda b, h, i: (b * nb + i, h)),
        scratch_shapes=[pltpu.VMEM((hp, nb, MOBA_BLOCK, HD_ATT), BF16),
                        pltpu.VMEM((hp, nb, HD_ATT, MOBA_BLOCK), BF16),
                        pltpu.VMEM((hp, nbp, HD_ATT), F32),
                        pltpu.VMEM((hp, 1, MOBA_BLOCK), F32),
                        pltpu.VMEM((hp, 1, MOBA_BLOCK), F32),
                        pltpu.VMEM((hp, HD_ATT, MOBA_BLOCK), F32)],
        compiler_params=_cparams("arbitrary", "arbitrary", "arbitrary"),
        name="moba_prompt",
    )(q, k, v)


MEAN_BLOCKS_PER_STEP = 4


def _block_mean_kernel(pt_ref, *refs):
    o_ref = refs[-1]
    for n in range(len(refs) // 2):
        tot = jnp.sum(refs[2 * n][...], axis=0) + jnp.sum(refs[2 * n + 1][...], axis=0)
        o_ref[n] = tot * (1.0 / MOBA_BLOCK)


def _block_means(cache_k, layer, page_table_flat):
    _, _, page, nh, hd = cache_k.shape
    n = page_table_flat.shape[0] // 2
    per = _tile(n, MEAN_BLOCKS_PER_STEP, 1)

    def page_map(e):
        return lambda s, pt: (layer, pt[2 * per * s + e], 0, 0, 0)

    return pl.pallas_call(
        _block_mean_kernel,
        out_shape=jax.ShapeDtypeStruct((n, nh, hd), F32),
        grid_spec=pltpu.PrefetchScalarGridSpec(
            num_scalar_prefetch=1,
            grid=(n // per,),
            in_specs=[pl.BlockSpec((None, None, page, nh, hd), page_map(e)) for e in range(2 * per)],
            out_specs=pl.BlockSpec((per, nh, hd), lambda s, pt: (s, 0, 0))),
        compiler_params=_cparams("arbitrary"),
        name="block_key_means",
    )(page_table_flat, *([cache_k] * (2 * per)))


def _select_kernel(km_ref, q_ref, o_ref, *, n_blocks, topk):
    q = q_ref[0]
    g = [jnp.sum(km_ref[0, n] * q, axis=1, keepdims=True) for n in range(n_blocks)]
    for s in range(topk):
        mx = g[0]
        for n in range(1, n_blocks):
            mx = jnp.maximum(mx, g[n])
        pick = jnp.full(mx.shape, n_blocks, jnp.int32)
        for n in range(n_blocks - 1, -1, -1):
            pick = jnp.where(g[n] == mx, n, pick)
        o_ref[0, s] = pick
        g = [jnp.where(pick == n, -jnp.inf, g[n]) for n in range(n_blocks)]


def _select_blocks(kmeans, q, n_seq, n_blocks, nh, topk):
    hd = HD_ATT
    kern = functools.partial(_select_kernel, n_blocks=n_blocks, topk=topk)
    return pl.pallas_call(
        kern,
        out_shape=jax.ShapeDtypeStruct((n_seq, topk, nh, 1), jnp.int32),
        grid=(n_seq,),
        in_specs=[pl.BlockSpec((1, n_blocks, nh, hd), lambda b: (b, 0, 0, 0)),
                  pl.BlockSpec((1, nh, hd), lambda b: (b, 0, 0))],
        out_specs=pl.BlockSpec((1, topk, nh, 1), lambda b: (b, 0, 0, 0)),
        compiler_params=_cparams("arbitrary"),
        name="moba_select",
    )(kmeans.reshape(n_seq, n_blocks, nh, hd), q.reshape(n_seq, nh, hd))


def _moba_sample_kernel(pt_ref, idx_ref, q_ref, kn_ref, vn_ref, ck_hbm, cv_hbm, o_ref, kbuf, vbuf, sem,
                        *, layer, scale, n_slots, n_pages, nh, page, n_seq):
    b = pl.program_id(0)
    n_pg = 2 * n_slots

    def copies(seq, slot):
        out = []
        for h in range(nh):
            for s in range(n_slots):
                blk = idx_ref[(seq * n_slots + s) * nh + h]
                for half in range(2):
                    pid = pt_ref[seq * n_pages + 2 * blk + half]
                    j = s * 2 + half
                    out.append(pltpu.make_async_copy(ck_hbm.at[layer, pid, :, h, :], kbuf.at[slot, h, j],
                                                     sem.at[0, slot]))
                    out.append(pltpu.make_async_copy(cv_hbm.at[layer, pid, :, h, :], vbuf.at[slot, h, j],
                                                     sem.at[1, slot]))
        return out

    @pl.when(b == 0)
    def _():
        for cp in copies(0, 0):
            cp.start()

    slot = b % 2

    @pl.when(b + 1 < n_seq)
    def _():
        for cp in copies(b + 1, 1 - slot):
            cp.start()

    for cp in copies(b, slot):
        cp.wait()

    for h in range(nh):
        q = q_ref[0, :, h * HD_ATT:(h + 1) * HD_ATT]
        kn = kn_ref[0, :, h * HD_ATT:(h + 1) * HD_ATT]
        vn = vn_ref[0, :, h * HD_ATT:(h + 1) * HD_ATT]
        kh = kbuf[slot, h].reshape(n_pg * page, HD_ATT)
        vh = vbuf[slot, h].reshape(n_pg * page, HD_ATT)
        s0 = jnp.sum(q * kn, axis=1, keepdims=True) * scale
        s = jnp.sum(kh * q, axis=1, keepdims=True) * scale
        m = jnp.maximum(jnp.max(s, axis=0, keepdims=True), s0)
        p = jnp.exp(s - m)
        p0 = jnp.exp(s0 - m)
        den = jnp.sum(p, axis=0, keepdims=True) + p0
        num = jnp.sum(p * vh, axis=0, keepdims=True) + p0 * vn
        o_ref[0, :, h * HD_ATT:(h + 1) * HD_ATT] = (num / den).astype(BF16)


def _moba_sample(q, k_new, v_new, cache_k, cache_v, layer, page_table_flat, sel_idx_flat, n_seq, n_pages, n_slots):
    _, _, page, nh, hd = cache_k.shape
    da = q.shape[1]
    assert MOBA_BLOCK == 2 * page and hd == HD_ATT
    kern = functools.partial(_moba_sample_kernel, layer=layer, scale=HD_ATT ** -0.5, n_slots=n_slots,
                             n_pages=n_pages, nh=nh, page=page, n_seq=n_seq)
    row = pl.BlockSpec((1, 1, da), lambda b, pt, idx: (b, 0, 0))
    out = pl.pallas_call(
        kern,
        out_shape=jax.ShapeDtypeStruct((n_seq, 1, da), BF16),
        grid_spec=pltpu.PrefetchScalarGridSpec(
            num_scalar_prefetch=2,
            grid=(n_seq,),
            in_specs=[row, row, row, pl.BlockSpec(memory_space=pl.ANY), pl.BlockSpec(memory_space=pl.ANY)],
            out_specs=row,
            scratch_shapes=[pltpu.VMEM((2, nh, 2 * n_slots, page, hd), F32),
                            pltpu.VMEM((2, nh, 2 * n_slots, page, hd), F32),
                            pltpu.SemaphoreType.DMA((2, 2))]),
        compiler_params=_cparams("arbitrary"),
        name="moba_sample",
    )(page_table_flat, sel_idx_flat, q.reshape(n_seq, 1, da), k_new.reshape(n_seq, 1, da),
      v_new.reshape(n_seq, 1, da), cache_k, cache_v)
    return out.reshape(n_seq, da)


def _shift_rows(x, tail_ref, first_ref, seq):
    tm = x.shape[0]
    local = lax.broadcasted_iota(jnp.int32, (tm, 1), 0)
    seq_off = lax.rem(pl.program_id(0) * tm, seq)
    prev = jnp.where(local == 0, tail_ref[7:8, :], pltpu.roll(x, 1, 0))
    return jnp.where((local + seq_off) == 0, first_ref[0], prev)


def _lora1_kernel(h_ref, hp_ref, *rest, seq):
    if seq is None:
        mu_ref, w1_ref, a1_ref, g1_ref, tw_ref, ta_ref, tg_ref = rest
        h = h_ref[...]
        hp = hp_ref[...]
    else:
        first_ref, mu_ref, w1_ref, a1_ref, g1_ref, tw_ref, ta_ref, tg_ref = rest
        h = h_ref[...]
        hp = _shift_rows(h, hp_ref, first_ref, seq)
    dx = hp - h
    mu = mu_ref[...]
    xw = (h + dx * mu[0:1, :]).astype(BF16)
    xa = (h + dx * mu[1:2, :]).astype(BF16)
    xg = (h + dx * mu[2:3, :]).astype(BF16)
    tw_ref[...] = jnp.tanh(_dot(xw, w1_ref[...])).astype(BF16)
    ta_ref[...] = _dot(xa, a1_ref[...]).astype(BF16)
    tg_ref[...] = jax.nn.sigmoid(_dot(xg, g1_ref[...])).astype(BF16)


def _prev_specs(tm, width, seq):
    tail = pl.BlockSpec((8, width), lambda i: (jnp.maximum(i * (tm // 8) - 1, 0), 0))
    first = pl.BlockSpec((1, 1, width), lambda i: (i * tm // seq, 0, 0))
    return tail, first


def _lora1(h, hprev, mu_wag, w1, a1, g1, seq=None):
    m, d = h.shape
    tm = _tile(m if seq is None else seq, 256, 16)
    nw, na, ng = w1.shape[1], a1.shape[1], g1.shape[1]
    row = lambda i: (i, 0)
    full = lambda i: (0, 0)
    if seq is None:
        prev_specs, prev_args = [pl.BlockSpec((tm, d), row)], [hprev]
    else:
        prev_specs, prev_args = list(_prev_specs(tm, d, seq)), [h, hprev]
    return pl.pallas_call(
        functools.partial(_lora1_kernel, seq=seq),
        out_shape=[jax.ShapeDtypeStruct((m, nw), BF16), jax.ShapeDtypeStruct((m, na), BF16),
                   jax.ShapeDtypeStruct((m, ng), BF16)],
        grid=(m // tm,),
        in_specs=[pl.BlockSpec((tm, d), row)] + prev_specs + [pl.BlockSpec(mu_wag.shape, full),
                  pl.BlockSpec((d, nw), full), pl.BlockSpec((d, na), full), pl.BlockSpec((d, ng), full)],
        out_specs=[pl.BlockSpec((tm, nw), row), pl.BlockSpec((tm, na), row), pl.BlockSpec((tm, ng), row)],
        compiler_params=_cparams("arbitrary"),
        name="rwkv_lora1",
    )(h, *prev_args, mu_wag, w1, a1, g1)


def _head_sums(x, bd):
    hi = x.astype(BF16)
    r1 = x - hi.astype(F32)
    mid = r1.astype(BF16)
    lo = (r1 - mid.astype(F32)).astype(BF16)
    return _dot(hi, bd) + _dot(mid, bd) + _dot(lo, bd)


def _block_diag_ones():
    a = lax.broadcasted_iota(jnp.int32, (LANES, LANES), 0) // HD_RWKV
    b = lax.broadcasted_iota(jnp.int32, (LANES, LANES), 1) // HD_RWKV
    return jnp.where(a == b, 1.0, 0.0).astype(BF16)


def _rwkv_prep_kernel(p_ref, pp_ref, *rest, seq):
    if seq is None:
        first_ref = None
        (tw_ref, ta_ref, tg_ref, w2_ref, a2_ref, g2_ref, mu_ref, vec_ref,
         r_ref, dec_ref, km_ref, v_ref, kk_ref, be_ref, g_ref, bo_ref) = rest
    else:
        (first_ref, tw_ref, ta_ref, tg_ref, w2_ref, a2_ref, g2_ref, mu_ref, vec_ref,
         r_ref, dec_ref, km_ref, v_ref, kk_ref, be_ref, g_ref, bo_ref) = rest
    dr = r_ref.shape[1]
    bd = _block_diag_ones()
    mu = mu_ref[...]
    vec = vec_ref[...]
    w0, a0, k_k, k_a, r_k = vec[0:1, :], vec[1:2, :], vec[2:3, :], vec[3:4, :], vec[4:5, :]
    p_all = p_ref[...]
    pp_all = pp_ref[...] if seq is None else _shift_rows(p_all, pp_ref, first_ref, seq)

    def shifted(c):
        pc = p_all[:, c * dr:(c + 1) * dr]
        return pc + (pp_all[:, c * dr:(c + 1) * dr] - pc) * mu[c:c + 1, :]

    r = shifted(0)
    k = shifted(1)
    v = shifted(2)
    w = -jnp.logaddexp(-(w0 + _dot(tw_ref[...], w2_ref[...])), 0.0) - 0.5
    dec = -jnp.exp(w)
    a = jax.nn.sigmoid(a0 + _dot(ta_ref[...], a2_ref[...]))
    g = _dot(tg_ref[...], g2_ref[...])
    kk = k * k_k
    kmod = k * (1.0 + (a - 1.0) * k_a)
    rk = r * kmod * r_k
    for c in range(dr // LANES):
        sl = slice(c * LANES, (c + 1) * LANES)
        kkc = kk[:, sl]
        kkc = kkc * lax.rsqrt(jnp.maximum(_head_sums(kkc * kkc, bd), 1e-24))
        kk_ref[:, sl] = kkc
        be_ref[:, sl] = kkc * a[:, sl]
        bo_ref[:, sl] = _head_sums(rk[:, sl], bd) * v[:, sl]
    r_ref[...] = r
    dec_ref[...] = dec
    km_ref[...] = kmod
    v_ref[...] = v
    g_ref[...] = g


def _rwkv_prep(p, pprev, tw, ta, tg, w2, a2, g2, mu_rkv, vecs, seq=None):
    m = p.shape[0]
    dr = p.shape[1] // 3
    tm = _tile(m if seq is None else seq, 128, 16)
    row = lambda i: (i, 0)
    full = lambda i: (0, 0)
    outs = [jax.ShapeDtypeStruct((m, dr), F32)] * 8
    if seq is None:
        prev_specs, prev_args = [pl.BlockSpec((tm, 3 * dr), row)], [pprev]
    else:
        prev_specs, prev_args = list(_prev_specs(tm, 3 * dr, seq)), [p, pprev]
    return pl.pallas_call(
        functools.partial(_rwkv_prep_kernel, seq=seq),
        out_shape=outs,
        grid=(m // tm,),
        in_specs=[pl.BlockSpec((tm, 3 * dr), row)] + prev_specs + [
                  pl.BlockSpec((tm, tw.shape[1]), row), pl.BlockSpec((tm, ta.shape[1]), row),
                  pl.BlockSpec((tm, tg.shape[1]), row),
                  pl.BlockSpec(w2.shape, full), pl.BlockSpec(a2.shape, full), pl.BlockSpec(g2.shape, full),
                  pl.BlockSpec(mu_rkv.shape, full), pl.BlockSpec(vecs.shape, full)],
        out_specs=[pl.BlockSpec((tm, dr), row)] * 8,
        compiler_params=_cparams("arbitrary"),
        name="rwkv_prep",
    )(p, *prev_args, tw, ta, tg, w2, a2, g2, mu_rkv, vecs)


SCAN_STEPS = 8


def _segment_ones(width):
    a = lax.broadcasted_iota(jnp.int32, (width, width), 0) // HD_RWKV
    b = lax.broadcasted_iota(jnp.int32, (width, width), 1) // HD_RWKV
    return jnp.where(a == b, 1.0, 0.0).astype(BF16)


def _rwkv_scan_kernel(kk_ref, dec_ref, be_ref, km_ref, r_ref, v_ref, s0_ref, y_ref, s_ref, *, tc, npair):
    c = pl.program_id(1)
    n = HD_RWKV
    gs = min(SCAN_STEPS, tc)
    wide = 2 if npair % 2 == 0 else 1
    assert tc % gs == 0

    @pl.when(c == 0)
    def _():
        s_ref[...] = s0_ref[...]

    seg = _segment_ones(wide * LANES)
    lane = lax.broadcasted_iota(jnp.int32, (n, LANES), 1)
    sub = lax.broadcasted_iota(jnp.int32, (n, LANES), 0)
    diag = (lane % n) == sub

    def stack(parts):
        rows = [jnp.concatenate(parts[i:i + wide], axis=1) for i in range(0, npair, wide)]
        return jnp.concatenate(rows, axis=0).astype(BF16)

    def part(full, p):
        return full[(p // wide) * n:(p // wide + 1) * n, (p % wide) * LANES:(p % wide + 1) * LANES]

    def steps(t0):
        def row(ref, p, r):
            return ref[pl.ds(t0, gs), p * LANES:(p + 1) * LANES][r:r + 1, :]

        y_rows = [[None] * gs for _ in range(npair)]

        def finish(y_all, r):
            for p in range(npair):
                y_rows[p][r] = jnp.sum(jnp.where(diag, part(y_all, p), 0.0), axis=0, keepdims=True)

        pending = None
        for r in range(gs):
            sa_all = _dot(stack([s_ref[0, p] * row(kk_ref, p, r) for p in range(npair)]), seg)
            vb_all = _dot(stack([jnp.where(diag, row(v_ref, p, r), 0.0) for p in range(npair)]), seg)
            if pending is not None:
                finish(pending, r - 1)
            ys = []
            for p in range(npair):
                s_new = (s_ref[0, p] * jnp.exp(row(dec_ref, p, r)) - part(sa_all, p) * row(be_ref, p, r)
                         + part(vb_all, p) * row(km_ref, p, r))
                s_ref[0, p] = s_new
                ys.append(s_new * row(r_ref, p, r))
            pending = _dot(stack(ys), seg)
        finish(pending, gs - 1)
        for p in range(npair):
            y_ref[pl.ds(t0, gs), p * LANES:(p + 1) * LANES] = jnp.concatenate(y_rows[p], axis=0)

    if tc == gs:
        steps(0)
    else:
        def body(i, carry):
            steps(pl.multiple_of(i * gs, gs))
            return carry

        lax.fori_loop(0, tc // gs, body, 0)


def _rwkv_scan(kk, dec, be, km, r, v, s0, batch, seq):
    m, dr = kk.shape
    npair = dr // LANES
    tc = _tile(seq, 128, 8)
    nc = seq // tc
    if seq == 1:
        kk, dec, be, km, r, v = (a.reshape(m, 1, dr) for a in (kk, dec, be, km, r, v))
        op = pl.BlockSpec((None, 1, dr), lambda b, c: (b, 0, 0))
        y_shape = jax.ShapeDtypeStruct((m, 1, dr), F32)
    else:
        op = pl.BlockSpec((tc, dr), lambda b, c: (b * nc + c, 0))
        y_shape = jax.ShapeDtypeStruct((m, dr), F32)
    kern = functools.partial(_rwkv_scan_kernel, tc=tc, npair=npair)
    st = pl.BlockSpec((1, npair, HD_RWKV, LANES), lambda b, c: (b, 0, 0, 0))
    y, s_fin = pl.pallas_call(
        kern,
        out_shape=[y_shape, jax.ShapeDtypeStruct((batch, npair, HD_RWKV, LANES), F32)],
        grid=(batch, nc),
        in_specs=[op] * 6 + [st],
        out_specs=[op, st],
        compiler_params=_cparams("arbitrary", "arbitrary"),
        name="rwkv_scan",
    )(kk, dec, be, km, r, v, s0)
    return y.reshape(m, dr), s_fin


CHUNK = 64


def _rwkv_chunk_kernel(kk_ref, ld_ref, be_ref, km_ref, r_ref, v_ref, s0_ref, y_ref, s_ref, *, npair):
    c = pl.program_id(1)
    L = CHUNK
    n = HD_RWKV

    @pl.when(c == 0)
    def _():
        s_ref[...] = s0_ref[...]

    lane = lax.broadcasted_iota(jnp.int32, (L, LANES), 1)
    trow = lax.broadcasted_iota(jnp.int32, (L, LANES), 0)
    head0 = lane < n
    src = lane % n
    strict = trow > src
    incl = trow >= src
    tri = (lax.broadcasted_iota(jnp.int32, (L, L), 0) >= lax.broadcasted_iota(jnp.int32, (L, L), 1))
    tri = jnp.where(tri, 1.0, 0.0).astype(BF16)
    bd_r = lax.broadcasted_iota(jnp.int32, (LANES, LANES), 0) // n
    bd_c = lax.broadcasted_iota(jnp.int32, (LANES, LANES), 1) // n
    bdmask = bd_r == bd_c

    def blk(x):
        return jnp.concatenate([jnp.where(head0, x, 0.0), jnp.where(head0, 0.0, x)], axis=0).astype(BF16)

    def bf(x):
        return x.astype(BF16)

    pairs = range(npair)
    sls = [slice(p * LANES, (p + 1) * LANES) for p in pairs]

    cum = []
    for p in pairs:
        ld = ld_ref[:, sls[p]]
        hi = bf(ld)
        r1 = ld - hi.astype(F32)
        mid = bf(r1)
        lo = bf(r1 - mid.astype(F32))
        cum.append(_dot(tri, hi) + _dot(tri, mid) + _dot(tri, lo))

    z, s0t = [], []
    for p in pairs:
        e_neg = jnp.exp(-cum[p])
        abar = -kk_ref[:, sls[p]] * jnp.exp(cum[p] - ld_ref[:, sls[p]])
        rbar = r_ref[:, sls[p]] * jnp.exp(cum[p])
        lhs = bf(jnp.concatenate([abar, rbar], axis=0))
        rhs = jnp.concatenate([blk(be_ref[:, sls[p]] * e_neg), blk(km_ref[:, sls[p]] * e_neg)], axis=0)
        z.append(_dot_nt(lhs, rhs))
        s0t.append(_dot_nt(lhs, bf(s_ref[0, p])))

    u, pw = [], []
    for p in pairs:
        amat = jnp.where(strict, z[p][:L, 2 * L:], 0.0)
        u.append(s0t[p][:L] + _dot(bf(amat), blk(v_ref[:, sls[p]])))
        pw.append(jnp.where(strict, z[p][:L, :2 * L], 0.0))

    n_fac = L.bit_length() - 1
    for it in range(n_fac):
        u = [u[p] + _dot(bf(pw[p]), blk(u[p])) for p in pairs]
        if it + 1 < n_fac:
            pw = [_dot(bf(pw[p]), blk(pw[p])) for p in pairs]

    for p in pairs:
        rbk = jnp.concatenate([jnp.where(incl, z[p][L:, :2 * L], 0.0), jnp.where(incl, z[p][L:, 2 * L:], 0.0)],
                              axis=1)
        uv = jnp.concatenate([blk(u[p]), blk(v_ref[:, sls[p]])], axis=0)
        y_ref[:, sls[p]] = s0t[p][L:] + _dot(bf(rbk), uv)

    for p in pairs:
        cl = cum[p][L - 1:L, :]
        e_end = jnp.exp(cl - cum[p])
        uv_t = jnp.transpose(jnp.concatenate([u[p], v_ref[:, sls[p]]], axis=0))
        ends = jnp.concatenate([be_ref[:, sls[p]] * e_end, km_ref[:, sls[p]] * e_end], axis=0)
        cmat = _dot(bf(uv_t), bf(ends))
        s_ref[0, p] = s_ref[0, p] * jnp.exp(cl) + jnp.where(bdmask, cmat, 0.0)


def _rwkv_chunked(kk, ld, be, km, r, v, s0, batch, seq):
    m, dr = kk.shape
    npair = dr // LANES
    assert seq % CHUNK == 0 and 2 * CHUNK == LANES
    nc = seq // CHUNK
    op = pl.BlockSpec((CHUNK, dr), lambda b, c: (b * nc + c, 0))
    st = pl.BlockSpec((1, npair, LANES, LANES), lambda b, c: (b, 0, 0, 0))
    return pl.pallas_call(
        functools.partial(_rwkv_chunk_kernel, npair=npair),
        out_shape=[jax.ShapeDtypeStruct((m, dr), F32), jax.ShapeDtypeStruct((batch, npair, LANES, LANES), F32)],
        grid=(batch, nc),
        in_specs=[op] * 6 + [st],
        out_specs=[op, st],
        compiler_params=_cparams("arbitrary", "arbitrary"),
        name="rwkv_chunked",
    )(kk, ld, be, km, r, v, s0)


def _rwkv_post_kernel(y_ref, bo_ref, g_ref, ln_ref, o_ref):
    bd = _block_diag_ones()
    ln = ln_ref[...]
    inv_n = 1.0 / HD_RWKV
    for c in range(y_ref.shape[1] // LANES):
        sl = slice(c * LANES, (c + 1) * LANES)
        y = y_ref[:, sl]
        mean = _head_sums(y, bd) * inv_n
        d = y - mean
        var = _head_sums(d * d, bd) * inv_n
        yn = d * lax.rsqrt(var + GN_EPS) * ln[0:1, sl] + ln[1:2, sl]
        o_ref[:, sl] = ((yn + bo_ref[:, sl]) * g_ref[:, sl]).astype(BF16)


def _rwkv_post(y, bonus, g, ln):
    m, dr = y.shape
    tm = _tile(m, 256, 16)
    row = lambda i: (i, 0)
    return pl.pallas_call(
        _rwkv_post_kernel,
        out_shape=jax.ShapeDtypeStruct((m, dr), BF16),
        grid=(m // tm,),
        in_specs=[pl.BlockSpec((tm, dr), row)] * 3 + [pl.BlockSpec(ln.shape, lambda i: (0, 0))],
        out_specs=pl.BlockSpec((tm, dr), row),
        compiler_params=_cparams("arbitrary"),
        name="rwkv_post",
    )(y, bonus, g, ln)


def _pack_state(s):
    b, h, n, _ = s.shape
    return s.reshape(b, h // 2, 2, n, n).transpose(0, 1, 3, 2, 4).reshape(b, h // 2, n, 2 * n)


def _unpack_state(s):
    b, p, n, _ = s.shape
    return s.reshape(b, p, n, 2, n).transpose(0, 1, 3, 2, 4).reshape(b, 2 * p, n, n)


def _blockdiag_state(s):
    b, h, n, _ = s.shape
    s = s.reshape(b, h // 2, 2, n, n)
    z = jnp.zeros_like(s[:, :, 0])
    top = jnp.concatenate([s[:, :, 0], z], axis=-1)
    bot = jnp.concatenate([z, s[:, :, 1]], axis=-1)
    return jnp.concatenate([top, bot], axis=-2)


def _blockdiag_to_heads(s):
    b, p, _, _ = s.shape
    n = HD_RWKV
    return jnp.stack([s[:, :, :n, :n], s[:, :, n:, n:]], axis=2).reshape(b, 2 * p, n, n)


def _pad_cols(w, n):
    return jnp.pad(w, ((0, 0), (0, n - w.shape[1])))


def _pad_rows(w, n):
    return jnp.pad(w, ((0, n - w.shape[0]), (0, 0)))


def _rope_tables(pos, rows_per_pos):
    half = HD_ATT // 2
    inv = ROPE_THETA ** (-jnp.arange(half, dtype=F32) / half)
    ang = pos.astype(F32)[:, None] * inv[None, :]
    cos = jnp.cos(ang)
    sin = jnp.sin(ang)
    cos2 = jnp.concatenate([cos, cos], axis=-1)
    sin2 = jnp.concatenate([-sin, sin], axis=-1)
    if rows_per_pos is not None:
        cos2 = jnp.broadcast_to(cos2, (rows_per_pos, HD_ATT))
        sin2 = jnp.broadcast_to(sin2, (rows_per_pos, HD_ATT))
    return cos2, sin2


def kernel(x_prompt, x_sample, cache_k, cache_v, state_shift, state_wkv, page_table, c_prompt, c_sample, g_norm, w_ada, b_ada, ffn_w_gate, ffn_w_up, ffn_w_down, w_in, w_out, mu_rkv, mu_wag, w0, w_lora1, w_lora2, a0, a_lora1, a_lora2, g_lora1, g_lora2, k_k, k_a, r_k, ln_x_w, ln_x_b, g_final):
    B, T, D = x_prompt.shape
    DB, DS, _ = x_sample.shape
    depth = g_norm.shape[0]
    assert DS == 1
    n_pages = page_table.shape[1]
    page = cache_k.shape[2]
    past_len = n_pages * page
    assert past_len % MOBA_BLOCK == 0 and MOBA_BLOCK == 2 * page
    da = cache_k.shape[3] * cache_k.shape[4]
    nh = da // HD_ATT
    dr = D - da
    n_slots = min(MOBA_TOPK, past_len // MOBA_BLOCK)

    xp = x_prompt.reshape(B * T, D)
    xs = x_sample.reshape(DB, D)
    pt_flat = page_table.reshape(-1).astype(jnp.int32)
    cos_p, sin_p = _rope_tables(jnp.arange(T, dtype=jnp.int32), None)
    cos_p = jnp.tile(cos_p, (B, 1))
    sin_p = jnp.tile(sin_p, (B, 1))

    outs = [[] for _ in range(8)]
    n_c = B + DB
    n_cp = -(-n_c // 8) * 8
    c_all = jnp.pad(jnp.concatenate([c_prompt, c_sample], axis=0), ((0, n_cp - n_c), (0, 0)))

    for l in range(depth):
        bf = lambda w: w.astype(BF16)
        wout = bf(w_out[l])
        lw = -(-w_lora1.shape[2] // LANES) * LANES
        la = -(-a_lora1.shape[2] // LANES) * LANES
        lg = -(-g_lora1.shape[2] // LANES) * LANES
        w1, a1, g1 = bf(_pad_cols(w_lora1[l], lw)), bf(_pad_cols(a_lora1[l], la)), bf(_pad_cols(g_lora1[l], lg))
        w2, a2, g2 = bf(_pad_rows(w_lora2[l], lw)), bf(_pad_rows(a_lora2[l], la)), bf(_pad_rows(g_lora2[l], lg))
        vecs = jnp.stack([w0[l], a0[l], k_k[l], k_a[l], r_k[l].reshape(dr)], axis=0)
        vecs = jnp.pad(vecs, ((0, 3), (0, 0)))
        ln = jnp.pad(jnp.stack([ln_x_w[l], ln_x_b[l]], axis=0), ((0, 6), (0, 0)))
        mu3 = jnp.pad(mu_rkv[l], ((0, 5), (0, 0)))
        muw = jnp.pad(mu_wag[l], ((0, 5), (0, 0)))

        mod = _ada(c_all, w_ada[l], b_ada[l])
        chunks = [mod[:, k * D:(k + 1) * D] for k in range(9)]
        mod_p = [ch[:B].reshape(B, 1, D) for ch in chunks]
        mod_s = [ch[B:B + DB].reshape(1, DB, D) for ch in chunks]

        def ffn(xp_, xs_, idx):
            sub = 2 * idx
            (hp_,) = _modulate(xp_, g_norm[l, sub], mod_p[3 * sub], mod_p[3 * sub + 1], T, False)
            (hs_,) = _modulate(xs_, g_norm[l, sub], mod_s[3 * sub], mod_s[3 * sub + 1], 1, False)
            u_p, wd_b, u_s = _ffn_up(hp_, ffn_w_gate, ffn_w_up, l, idx, ffn_w_down, rider=hs_)
            return _ffn_down(u_p, wd_b, xp_, mod_p[3 * sub + 2], T,
                             rider=(u_s, xs_, mod_s[3 * sub + 2].reshape(DB, D)))

        xp, xs = ffn(xp, xs, 0)

        hb, hf = _modulate(xp, g_norm[l, 1], mod_p[3], mod_p[4], T, True)
        hbs, hfs = _modulate(xs, g_norm[l, 1], mod_s[3], mod_s[4], 1, True)
        n_rid = 2 * DB + B
        n_rid_p = -(-n_rid // 16) * 16
        riders = jnp.concatenate([hbs, bf(state_shift[l]), jnp.zeros((n_rid_p - 2 * DB, D), BF16)], axis=0)
        cos_r, sin_r = _rope_tables(jnp.full((1,), past_len, dtype=jnp.int32), n_rid_p)
        q, q_r = _proj(hb, w_in, l, 0, da, (cos_p, sin_p), riders, (cos_r, sin_r))
        k, k_r = _proj(hb, w_in, l, da, da, (cos_p, sin_p), riders, (cos_r, sin_r))
        v, v_r = _proj(hb, w_in, l, 2 * da, da, None, riders)
        prk, p_r = _proj(hb, w_in, l, 3 * da, 3 * dr, None, riders)
        qs, ks, vs, prk_s = q_r[:DB], k_r[:DB], v_r[:DB], p_r[:DB]
        p_first = p_r[DB:2 * DB + B]
        att = _moba_prompt(q, k, v, B, T)

        h3 = hf.reshape(B, T, D)

        def rwkv(hf_, hprev_, prk_, pprev_, s0, nb_, nt_):
            seq = None if nt_ == 1 else nt_
            tw, ta, tg = _lora1(hf_, hprev_, muw, w1, a1, g1, seq)
            r_, dec_, km_, v_, kk_, be_, g_, bo_ = _rwkv_prep(prk_, pprev_, tw, ta, tg, w2, a2, g2, mu3, vecs, seq)
            if nt_ % CHUNK == 0:
                y, s_fin = _rwkv_chunked(kk_, dec_, be_, km_, r_, v_, _blockdiag_state(s0), nb_, nt_)
                s_fin = _blockdiag_to_heads(s_fin)
            else:
                y, s_fin = _rwkv_scan(kk_, dec_, be_, km_, r_, v_, _pack_state(s0), nb_, nt_)
                s_fin = _unpack_state(s_fin)
            rw = _rwkv_post(y, bo_, g_, ln)
            return rw, s_fin

        s0_p = jnp.zeros((B, dr // HD_RWKV, HD_RWKV, HD_RWKV), F32)
        rw_p, sfin_p = rwkv(hf, jnp.zeros((B, 1, D), F32), prk, p_first[DB:DB + B].reshape(B, 1, 3 * dr),
                            s0_p, B, T)
        xp = _out_proj(att, rw_p, wout, xp, mod_p[5], T)

        n_blocks = past_len // MOBA_BLOCK
        kmeans = _block_means(cache_k, l, pt_flat)
        sel = _select_blocks(kmeans, qs, DB, n_blocks, nh, n_slots)
        att_s = _moba_sample(qs, ks, vs, cache_k, cache_v, l, pt_flat, sel.reshape(-1), DB, n_pages, n_slots)

        rw_s, sfin_s = rwkv(hfs, state_shift[l], prk_s, p_first[:DB], state_wkv[l], DB, 1)
        xs = _out_proj(att_s, rw_s, wout, xs, mod_s[5], 1)
        xp, xs = ffn(xp, xs, 1)

        outs[0].append(k.reshape(B, T, nh, HD_ATT))
        outs[1].append(v.reshape(B, T, nh, HD_ATT))
        outs[2].append(h3[:, -1])
        outs[3].append(sfin_p)
        outs[4].append(ks.reshape(DB, 1, nh, HD_ATT))
        outs[5].append(vs.reshape(DB, 1, nh, HD_ATT))
        outs[6].append(hfs)
        outs[7].append(sfin_s)

    y_prompt = _rmsnorm(xp, g_final).reshape(B, T, D)
    y_sample = _rmsnorm(xs, g_final).reshape(DB, 1, D)
    return (y_prompt, y_sample) + tuple(jnp.stack(o) for o in outs)
```

```python
import functools

import jax
import jax.numpy as jnp
from jax import lax
from jax.experimental import pallas as pl
from jax.experimental.pallas import tpu as pltpu

F32 = jnp.float32
BF16 = jnp.bfloat16

HD_ATT = 128
HD_RWKV = 64
LANES = 128
MOBA_BLOCK = 256
MOBA_TOPK = 3
ROPE_THETA = 10000.0
RMS_EPS = 1e-6
GN_EPS = 64e-5
NEG_BIG = -1e30
VMEM_LIMIT = 56 * 1024 * 1024


def _cparams(*sem):
    return pltpu.CompilerParams(dimension_semantics=sem, vmem_limit_bytes=VMEM_LIMIT)


def _tile(n, pref, align):
    if n <= pref:
        return n
    t = (pref // align) * align
    while t >= align:
        if n % t == 0:
            return t
        t -= align
    return n


def _dot(a, b):
    return jnp.dot(a, b, preferred_element_type=F32)


def _dot_nt(a, b):
    return lax.dot_general(a, b, (((1,), (1,)), ((), ())), preferred_element_type=F32)


def _ada_kernel(c_ref, w_ref, b_ref, o_ref):
    c = c_ref[...]
    s = (c * jax.nn.sigmoid(c)).astype(BF16)
    o_ref[...] = _dot(s, w_ref[...].astype(BF16)) + b_ref[...]


def _ada(c, w, b):
    m, d = c.shape
    n = w.shape[1]
    tn = _tile(n, 512, LANES)
    return pl.pallas_call(
        _ada_kernel,
        out_shape=jax.ShapeDtypeStruct((m, n), F32),
        grid=(n // tn,),
        in_specs=[pl.BlockSpec((m, d), lambda j: (0, 0)),
                  pl.BlockSpec((d, tn), lambda j: (0, j)),
                  pl.BlockSpec((1, tn), lambda j: (0, j))],
        out_specs=pl.BlockSpec((m, tn), lambda j: (0, j)),
        compiler_params=_cparams("arbitrary"),
        name="ada_proj",
    )(c, w, b.reshape(1, n))


def _modulate_kernel(x_ref, g_ref, sh_ref, sc_ref, *o_refs):
    x = x_ref[...]
    y = x * lax.rsqrt(jnp.mean(x * x, axis=-1, keepdims=True) + RMS_EPS) * g_ref[...]
    h = y * (1.0 + sc_ref[0]) + sh_ref[0]
    o_refs[0][...] = h.astype(BF16)
    if len(o_refs) > 1:
        o_refs[1][...] = h


def _modulate(x, g, sh, sc, rows_per_group, want_f32):
    m, d = x.shape
    r = sh.shape[1]
    tm = m if r > 1 else _tile(rows_per_group, 256, 16)
    grp = (lambda i: (0, 0, 0)) if r > 1 else (lambda i: (i * tm // rows_per_group, 0, 0))
    out_shape = [jax.ShapeDtypeStruct((m, d), BF16)]
    out_specs = [pl.BlockSpec((tm, d), lambda i: (i, 0))]
    if want_f32:
        out_shape.append(jax.ShapeDtypeStruct((m, d), F32))
        out_specs.append(pl.BlockSpec((tm, d), lambda i: (i, 0)))
    return pl.pallas_call(
        _modulate_kernel,
        out_shape=out_shape,
        grid=(m // tm,),
        in_specs=[pl.BlockSpec((tm, d), lambda i: (i, 0)),
                  pl.BlockSpec((1, d), lambda i: (0, 0)),
                  pl.BlockSpec((1, r, d), grp),
                  pl.BlockSpec((1, r, d), grp)],
        out_specs=out_specs,
        compiler_params=_cparams("arbitrary"),
        name="modulate",
    )(x, g.reshape(1, d), sh, sc)


def _rmsnorm_kernel(x_ref, g_ref, o_ref):
    x = x_ref[...]
    o_ref[...] = x * lax.rsqrt(jnp.mean(x * x, axis=-1, keepdims=True) + RMS_EPS) * g_ref[...]


def _rmsnorm(x, g):
    m, d = x.shape
    tm = _tile(m, 256, 8)
    return pl.pallas_call(
        _rmsnorm_kernel,
        out_shape=jax.ShapeDtypeStruct((m, d), F32),
        grid=(m // tm,),
        in_specs=[pl.BlockSpec((tm, d), lambda i: (i, 0)), pl.BlockSpec((1, d), lambda i: (0, 0))],
        out_specs=pl.BlockSpec((tm, d), lambda i: (i, 0)),
        compiler_params=_cparams("arbitrary"),
        name="final_rmsnorm",
    )(x, g.reshape(1, d))


def _first_pass(nj):
    return lambda i, j: jnp.where(i == 0, j, nj - 1)


def _ffn_up_kernel(h_ref, wg_ref, wu_ref, *rest, has_wd, has_rider):
    rest = list(rest)
    wd_ref = rest.pop(0) if has_wd else None
    hs_ref = rest.pop(0) if has_rider else None
    o_ref = rest.pop(0)
    wdb_ref = rest.pop(0) if has_wd else None
    us_ref = rest.pop(0) if has_rider else None

    def gated(h):
        g = _dot(h, wg_ref[...].astype(BF16))
        return (g * jax.nn.sigmoid(g) * _dot(h, wu_ref[...].astype(BF16))).astype(BF16)

    if has_wd or has_rider:
        @pl.when(pl.program_id(0) == 0)
        def _():
            if has_wd:
                wdb_ref[...] = wd_ref[...].astype(BF16)
            if has_rider:
                us_ref[...] = gated(hs_ref[...])

    o_ref[...] = gated(h_ref[...])


def _ffn_up(h, wg_all, wu_all, layer, idx, wd_all=None, rider=None):
    m, d = h.shape
    f = wg_all.shape[-1]
    tm = _tile(m, 1024, 16)
    tn = _tile(f, 256, LANES)
    nj = f // tn
    fp = _first_pass(nj)
    w_spec = pl.BlockSpec((None, None, d, tn), lambda i, j: (layer, idx, 0, j))
    in_specs = [pl.BlockSpec((tm, d), lambda i, j: (i, 0)), w_spec, w_spec]
    out_shape = [jax.ShapeDtypeStruct((m, f), BF16)]
    out_specs = [pl.BlockSpec((tm, tn), lambda i, j: (i, j))]
    args = [h, wg_all, wu_all]
    if wd_all is not None:
        in_specs.append(pl.BlockSpec((None, None, tn, d), lambda i, j: (layer, idx, fp(i, j), 0)))
        out_shape.append(jax.ShapeDtypeStruct((f, d), BF16))
        out_specs.append(pl.BlockSpec((tn, d), lambda i, j: (fp(i, j), 0)))
        args.append(wd_all)
    if rider is not None:
        ms = rider.shape[0]
        in_specs.append(pl.BlockSpec((ms, d), lambda i, j: (0, 0)))
        out_shape.append(jax.ShapeDtypeStruct((ms, f), BF16))
        out_specs.append(pl.BlockSpec((ms, tn), lambda i, j: (0, fp(i, j))))
        args.append(rider)
    res = pl.pallas_call(
        functools.partial(_ffn_up_kernel, has_wd=wd_all is not None, has_rider=rider is not None),
        out_shape=out_shape,
        grid=(m // tm, nj),
        in_specs=in_specs,
        out_specs=out_specs,
        compiler_params=_cparams("arbitrary", "arbitrary"),
        name="ffn_up",
    )(*args)
    return res if len(res) > 1 else res[0]


def _ffn_down_kernel(u_ref, w_ref, x_ref, ga_ref, *rest):
    if len(rest) == 1:
        (o_ref,) = rest
    else:
        us_ref, xs_ref, gas_ref, o_ref, os_ref = rest

        @pl.when(pl.program_id(0) == 0)
        def _():
            os_ref[...] = xs_ref[...] + 0.5 * gas_ref[...] * _dot(us_ref[...], w_ref[...])

    o_ref[...] = x_ref[...] + 0.5 * ga_ref[0] * _dot(u_ref[...], w_ref[...])


def _ffn_down(u, wd, x, ga, rows_per_group, rider=None):
    m, f = u.shape
    d = wd.shape[1]
    r = ga.shape[1]
    tm = m if r > 1 else _tile(rows_per_group, 512, 16)
    tn = _tile(d, 256, LANES)
    nj = d // tn
    grp = (lambda i, j: (0, 0, j)) if r > 1 else (lambda i, j: (i * tm // rows_per_group, 0, j))
    in_specs = [pl.BlockSpec((tm, f), lambda i, j: (i, 0)),
                pl.BlockSpec((f, tn), lambda i, j: (0, j)),
                pl.BlockSpec((tm, tn), lambda i, j: (i, j)),
                pl.BlockSpec((1, r, tn), grp)]
    out_shape = [jax.ShapeDtypeStruct((m, d), F32)]
    out_specs = [pl.BlockSpec((tm, tn), lambda i, j: (i, j))]
    args = [u, wd, x, ga]
    if rider is not None:
        ms = rider[0].shape[0]
        fp = _first_pass(nj)
        tile_s = pl.BlockSpec((ms, tn), lambda i, j: (0, fp(i, j)))
        in_specs += [pl.BlockSpec((ms, f), lambda i, j: (0, 0)), tile_s, tile_s]
        out_shape.append(jax.ShapeDtypeStruct((ms, d), F32))
        out_specs.append(tile_s)
        args += list(rider)
    res = pl.pallas_call(
        _ffn_down_kernel,
        out_shape=out_shape,
        grid=(m // tm, nj),
        in_specs=in_specs,
        out_specs=out_specs,
        compiler_params=_cparams("arbitrary", "arbitrary"),
        name="ffn_down",
    )(*args)
    return res if rider is not None else res[0]


def _rope_store(o_ref, x, cos, sin):
    for c in range(x.shape[1] // HD_ATT):
        xc = x[:, c * HD_ATT:(c + 1) * HD_ATT]
        o_ref[:, c * HD_ATT:(c + 1) * HD_ATT] = xc * cos + pltpu.roll(xc, HD_ATT // 2, 1) * sin


def _proj_kernel(h_ref, w_ref, *rest, has_rope, has_rider):
    rest = list(rest)
    cos_ref, sin_ref = (rest.pop(0), rest.pop(0)) if has_rope else (None, None)
    hs_ref = rest.pop(0) if has_rider else None
    cs_ref, ss_ref = (rest.pop(0), rest.pop(0)) if has_rope and has_rider else (None, None)
    o_ref = rest.pop(0)
    os_ref = rest.pop(0) if has_rider else None

    def emit(out_ref, rows, cos_r, sin_r):
        x = _dot(rows, w_ref[...].astype(BF16))
        if has_rope:
            _rope_store(out_ref, x, cos_r[...], sin_r[...])
        else:
            out_ref[...] = x

    if has_rider:
        @pl.when(pl.program_id(0) == 0)
        def _():
            emit(os_ref, hs_ref[...], cs_ref, ss_ref)

    emit(o_ref, h_ref[...], cos_ref, sin_ref)


def _proj(h, w, layer, col0, ncols, rope=None, rider=None, rider_rope=None):
    m, d = h.shape
    tm = _tile(m, 1024, 16)
    tn = _tile(ncols, 512, LANES)
    assert col0 % tn == 0
    j0 = col0 // tn
    nj = ncols // tn
    in_specs = [pl.BlockSpec((tm, d), lambda i, j: (i, 0)),
                pl.BlockSpec((None, d, tn), lambda i, j: (layer, 0, j + j0))]
    args = [h, w]
    out_shape = [jax.ShapeDtypeStruct((m, ncols), F32)]
    out_specs = [pl.BlockSpec((tm, tn), lambda i, j: (i, j))]
    if rope is not None:
        in_specs += [pl.BlockSpec((tm, HD_ATT), lambda i, j: (i, 0))] * 2
        args += list(rope)
    if rider is not None:
        ms = rider.shape[0]
        fp = _first_pass(nj)
        in_specs.append(pl.BlockSpec((ms, d), lambda i, j: (0, 0)))
        args.append(rider)
        if rope is not None:
            in_specs += [pl.BlockSpec((ms, HD_ATT), lambda i, j: (0, 0))] * 2
            args += list(rider_rope)
        out_shape.append(jax.ShapeDtypeStruct((ms, ncols), F32))
        out_specs.append(pl.BlockSpec((ms, tn), lambda i, j: (0, fp(i, j))))
    res = pl.pallas_call(
        functools.partial(_proj_kernel, has_rope=rope is not None, has_rider=rider is not None),
        out_shape=out_shape,
        grid=(m // tm, nj),
        in_specs=in_specs,
        out_specs=out_specs,
        compiler_params=_cparams("arbitrary", "arbitrary"),
        name="in_proj_rope" if rope is not None else "in_proj",
    )(*args)
    return res if rider is not None else res[0]


def _out_proj_kernel(att_ref, rw_ref, wa_ref, wr_ref, x_ref, ga_ref, o_ref):
    y = _dot(att_ref[...], wa_ref[...]) + _dot(rw_ref[...], wr_ref[...])
    o_ref[...] = x_ref[...] + ga_ref[0] * y


def _out_proj(att, rw, w, x, ga, rows_per_group):
    m, da = att.shape
    dr = rw.shape[1]
    d = w.shape[1]
    assert da == dr
    r = ga.shape[1]
    tm = m if r > 1 else _tile(rows_per_group, 1024, 16)
    tn = _tile(d, 512, LANES)
    grp = (lambda i, j: (0, 0, j)) if r > 1 else (lambda i, j: (i * tm // rows_per_group, 0, j))
    return pl.pallas_call(
        _out_proj_kernel,
        out_shape=jax.ShapeDtypeStruct((m, d), F32),
        grid=(m // tm, d // tn),
        in_specs=[pl.BlockSpec((tm, da), lambda i, j: (i, 0)),
                  pl.BlockSpec((tm, dr), lambda i, j: (i, 0)),
                  pl.BlockSpec((da, tn), lambda i, j: (0, j)),
                  pl.BlockSpec((dr, tn), lambda i, j: (1, j)),
                  pl.BlockSpec((tm, tn), lambda i, j: (i, j)),
                  pl.BlockSpec((1, r, tn), grp)],
        out_specs=pl.BlockSpec((tm, tn), lambda i, j: (i, j)),
        compiler_params=_cparams("arbitrary", "arbitrary"),
        name="out_proj",
    )(att, rw, w, w, x, ga)


def _rank_select(g_rows, n_valid, topk):
    nb = len(g_rows)
    sel = []
    for j in range(nb):
        rank = jnp.zeros_like(g_rows[j])
        for n in range(nb):
            if n == j:
                continue
            beats = (g_rows[n] > g_rows[j]) if n > j else (g_rows[n] >= g_rows[j])
            rank = rank + jnp.where(beats, jnp.where(n < n_valid, 1.0, 0.0), 0.0)
        sel.append(rank < float(topk))
    return sel


MOBA_HEADS_PER_STEP = 8


def _moba_prompt_kernel(q_ref, k_ref, v_ref, o_ref, kb_ref, vt_ref, km_ref, m_ref, l_ref, acc_ref,
                        *, nb, scale, hp):
    i = pl.program_id(2)
    blk = MOBA_BLOCK
    heads = [slice(hh * HD_ATT, (hh + 1) * HD_ATT) for hh in range(hp)]

    @pl.when(i == 0)
    def _():
        km_ref[...] = jnp.zeros_like(km_ref)
        for hh in range(hp):
            for j in range(nb):
                kj = k_ref[j * blk:(j + 1) * blk, heads[hh]]
                km_ref[hh, j:j + 1, :] = jnp.mean(kj, axis=0, keepdims=True)
                kb_ref[hh, j] = kj.astype(BF16)
                vt_ref[hh, j] = jnp.transpose(v_ref[j * blk:(j + 1) * blk, heads[hh]]).astype(BF16)

    kpos = lax.broadcasted_iota(jnp.int32, (blk, blk), 0)
    qpos = lax.broadcasted_iota(jnp.int32, (blk, blk), 1)
    hs = range(hp)
    qf = [q_ref[:, heads[hh]] for hh in hs]
    gt = [_dot_nt(km_ref[hh].astype(BF16), qf[hh].astype(BF16)) for hh in hs]
    sel = [_rank_select([gt[hh][j:j + 1, :] for j in range(nb)], i, MOBA_TOPK) for hh in hs]
    qb = [(qf[hh] * scale).astype(BF16) for hh in hs]

    s_own = [_dot_nt(kb_ref[hh, i], qb[hh]) for hh in hs]
    p_own = []
    for hh in hs:
        s = jnp.where(kpos <= qpos, s_own[hh], NEG_BIG)
        m0 = jnp.max(s, axis=0, keepdims=True)
        p = jnp.exp(s - m0)
        m_ref[hh] = m0
        l_ref[hh] = jnp.sum(p, axis=0, keepdims=True)
        p_own.append(p.astype(BF16))
    for hh in hs:
        acc_ref[hh] = _dot(vt_ref[hh, i], p_own[hh])

    for j in range(nb - 1):
        @pl.when(j < i)
        def _(j=j):
            s_all = [_dot_nt(kb_ref[hh, j], qb[hh]) for hh in hs]
            p_all, alphas = [], []
            for hh in hs:
                s = jnp.where(sel[hh][j], s_all[hh], NEG_BIG)
                m_old = m_ref[hh]
                m_new = jnp.maximum(m_old, jnp.max(s, axis=0, keepdims=True))
                alpha = jnp.exp(m_old - m_new)
                p = jnp.exp(s - m_new)
                m_ref[hh] = m_new
                l_ref[hh] = alpha * l_ref[hh] + jnp.sum(p, axis=0, keepdims=True)
                p_all.append(p.astype(BF16))
                alphas.append(alpha)
            for hh in hs:
                acc_ref[hh] = alphas[hh] * acc_ref[hh] + _dot(vt_ref[hh, j], p_all[hh])

    for hh in hs:
        o_ref[:, heads[hh]] = jnp.transpose(acc_ref[hh] / l_ref[hh]).astype(BF16)


def _moba_prompt(q, k, v, batch, seq):
    m, da = q.shape
    nh = da // HD_ATT
    assert seq % MOBA_BLOCK == 0
    nb = seq // MOBA_BLOCK
    nbp = -(-nb // 8) * 8
    hp = _tile(nh, MOBA_HEADS_PER_STEP, 1)
    w = hp * HD_ATT
    kern = functools.partial(_moba_prompt_kernel, nb=nb, scale=HD_ATT ** -0.5, hp=hp)
    return pl.pallas_call(
        kern,
        out_shape=jax.ShapeDtypeStruct((m, da), BF16),
        grid=(batch, nh // hp, nb),
        in_specs=[pl.BlockSpec((MOBA_BLOCK, w), lambda b, h, i: (b * nb + i, h)),
                  pl.BlockSpec((seq, w), lambda b, h, i: (b, h)),
                  pl.BlockSpec((seq, w), lambda b, h, i: (b, h))],
        out_specs=pl.BlockSpec((MOBA_BLOCK, w), lambda b, h, i: (b * nb + i, h)),
        scratch_shapes=[pltpu.VMEM((hp, nb, MOBA_BLOCK, HD_ATT), BF16),
                        pltpu.VMEM((hp, nb, HD_ATT, MOBA_BLOCK), BF16),
                        pltpu.VMEM((hp, nbp, HD_ATT), F32),
                        pltpu.VMEM((hp, 1, MOBA_BLOCK), F32),
                        pltpu.VMEM((hp, 1, MOBA_BLOCK), F32),
                        pltpu.VMEM((hp, HD_ATT, MOBA_BLOCK), F32)],
        compiler_params=_cparams("arbitrary", "arbitrary", "arbitrary"),
        name="moba_prompt",
    )(q, k, v)


MEAN_BLOCKS_PER_STEP = 4


def _block_mean_kernel(pt_ref, *refs):
    o_ref = refs[-1]
    for n in range(len(refs) // 2):
        tot = jnp.sum(refs[2 * n][...], axis=0) + jnp.sum(refs[2 * n + 1][...], axis=0)
        o_ref[n] = tot * (1.0 / MOBA_BLOCK)


def _block_means(cache_k, layer, page_table_flat):
    _, _, page, nh, hd = cache_k.shape
    n = page_table_flat.shape[0] // 2
    per = _tile(n, MEAN_BLOCKS_PER_STEP, 1)

    def page_map(e):
        return lambda s, pt: (layer, pt[2 * per * s + e], 0, 0, 0)

    return pl.pallas_call(
        _block_mean_kernel,
        out_shape=jax.ShapeDtypeStruct((n, nh, hd), F32),
        grid_spec=pltpu.PrefetchScalarGridSpec(
            num_scalar_prefetch=1,
            grid=(n // per,),
            in_specs=[pl.BlockSpec((None, None, page, nh, hd), page_map(e)) for e in range(2 * per)],
            out_specs=pl.BlockSpec((per, nh, hd), lambda s, pt: (s, 0, 0))),
        compiler_params=_cparams("arbitrary"),
        name="block_key_means",
    )(page_table_flat, *([cache_k] * (2 * per)))


def _select_kernel(km_ref, q_ref, o_ref, *, n_blocks, topk):
    q = q_ref[0]
    g = [jnp.sum(km_ref[0, n] * q, axis=1, keepdims=True) for n in range(n_blocks)]
    for s in range(topk):
        mx = g[0]
        for n in range(1, n_blocks):
            mx = jnp.maximum(mx, g[n])
        pick = jnp.full(mx.shape, n_blocks, jnp.int32)
        for n in range(n_blocks - 1, -1, -1):
            pick = jnp.where(g[n] == mx, n, pick)
        o_ref[0, s] = pick
        g = [jnp.where(pick == n, -jnp.inf, g[n]) for n in range(n_blocks)]


def _select_blocks(kmeans, q, n_seq, n_blocks, nh, topk):
    hd = HD_ATT
    kern = functools.partial(_select_kernel, n_blocks=n_blocks, topk=topk)
    return pl.pallas_call(
        kern,
        out_shape=jax.ShapeDtypeStruct((n_seq, topk, nh, 1), jnp.int32),
        grid=(n_seq,),
        in_specs=[pl.BlockSpec((1, n_blocks, nh, hd), lambda b: (b, 0, 0, 0)),
                  pl.BlockSpec((1, nh, hd), lambda b: (b, 0, 0))],
        out_specs=pl.BlockSpec((1, topk, nh, 1), lambda b: (b, 0, 0, 0)),
        compiler_params=_cparams("arbitrary"),
        name="moba_select",
    )(kmeans.reshape(n_seq, n_blocks, nh, hd), q.reshape(n_seq, nh, hd))


def _moba_sample_kernel(pt_ref, idx_ref, q_ref, kn_ref, vn_ref, ck_hbm, cv_hbm, o_ref, kbuf, vbuf, sem,
                        *, layer, scale, n_slots, n_pages, nh, page, n_seq):
    b = pl.program_id(0)
    n_pg = 2 * n_slots

    def copies(seq, slot):
        out = []
        for h in range(nh):
            for s in range(n_slots):
                blk = idx_ref[(seq * n_slots + s) * nh + h]
                for half in range(2):
                    pid = pt_ref[seq * n_pages + 2 * blk + half]
                    j = s * 2 + half
                    out.append(pltpu.make_async_copy(ck_hbm.at[layer, pid, :, h, :], kbuf.at[slot, h, j],
                                                     sem.at[0, slot]))
                    out.append(pltpu.make_async_copy(cv_hbm.at[layer, pid, :, h, :], vbuf.at[slot, h, j],
                                                     sem.at[1, slot]))
        return out

    @pl.when(b == 0)
    def _():
        for cp in copies(0, 0):
            cp.start()

    slot = b % 2

    @pl.when(b + 1 < n_seq)
    def _():
        for cp in copies(b + 1, 1 - slot):
            cp.start()

    for cp in copies(b, slot):
        cp.wait()

    for h in range(nh):
        q = q_ref[0, :, h * HD_ATT:(h + 1) * HD_ATT]
        kn = kn_ref[0, :, h * HD_ATT:(h + 1) * HD_ATT]
        vn = vn_ref[0, :, h * HD_ATT:(h + 1) * HD_ATT]
        kh = kbuf[slot, h].reshape(n_pg * page, HD_ATT)
        vh = vbuf[slot, h].reshape(n_pg * page, HD_ATT)
        s0 = jnp.sum(q * kn, axis=1, keepdims=True) * scale
        s = jnp.sum(kh * q, axis=1, keepdims=True) * scale
        m = jnp.maximum(jnp.max(s, axis=0, keepdims=True), s0)
        p = jnp.exp(s - m)
        p0 = jnp.exp(s0 - m)
        den = jnp.sum(p, axis=0, keepdims=True) + p0
        num = jnp.sum(p * vh, axis=0, keepdims=True) + p0 * vn
        o_ref[0, :, h * HD_ATT:(h + 1) * HD_ATT] = (num / den).astype(BF16)


def _moba_sample(q, k_new, v_new, cache_k, cache_v, layer, page_table_flat, sel_idx_flat, n_seq, n_pages, n_slots):
    _, _, page, nh, hd = cache_k.shape
    da = q.shape[1]
    assert MOBA_BLOCK == 2 * page and hd == HD_ATT
    kern = functools.partial(_moba_sample_kernel, layer=layer, scale=HD_ATT ** -0.5, n_slots=n_slots,
                             n_pages=n_pages, nh=nh, page=page, n_seq=n_seq)
    row = pl.BlockSpec((1, 1, da), lambda b, pt, idx: (b, 0, 0))
    out = pl.pallas_call(
        kern,
        out_shape=jax.ShapeDtypeStruct((n_seq, 1, da), BF16),
        grid_spec=pltpu.PrefetchScalarGridSpec(
            num_scalar_prefetch=2,
            grid=(n_seq,),
            in_specs=[row, row, row, pl.BlockSpec(memory_space=pl.ANY), pl.BlockSpec(memory_space=pl.ANY)],
            out_specs=row,
            scratch_shapes=[pltpu.VMEM((2, nh, 2 * n_slots, page, hd), F32),
                            pltpu.VMEM((2, nh, 2 * n_slots, page, hd), F32),
                            pltpu.SemaphoreType.DMA((2, 2))]),
        compiler_params=_cparams("arbitrary"),
        name="moba_sample",
    )(page_table_flat, sel_idx_flat, q.reshape(n_seq, 1, da), k_new.reshape(n_seq, 1, da),
      v_new.reshape(n_seq, 1, da), cache_k, cache_v)
    return out.reshape(n_seq, da)


def _shift_rows(x, tail_ref, first_ref, seq):
    tm = x.shape[0]
    local = lax.broadcasted_iota(jnp.int32, (tm, 1), 0)
    seq_off = lax.rem(pl.program_id(0) * tm, seq)
    prev = jnp.where(local == 0, tail_ref[7:8, :], pltpu.roll(x, 1, 0))
    return jnp.where((local + seq_off) == 0, first_ref[0], prev)


def _lora1_kernel(h_ref, hp_ref, *rest, seq):
    if seq is None:
        mu_ref, w1_ref, a1_ref, g1_ref, tw_ref, ta_ref, tg_ref = rest
        h = h_ref[...]
        hp = hp_ref[...]
    else:
        first_ref, mu_ref, w1_ref, a1_ref, g1_ref, tw_ref, ta_ref, tg_ref = rest
        h = h_ref[...]
        hp = _shift_rows(h, hp_ref, first_ref, seq)
    dx = hp - h
    mu = mu_ref[...]
    xw = (h + dx * mu[0:1, :]).astype(BF16)
    xa = (h + dx * mu[1:2, :]).astype(BF16)
    xg = (h + dx * mu[2:3, :]).astype(BF16)
    tw_ref[...] = jnp.tanh(_dot(xw, w1_ref[...])).astype(BF16)
    ta_ref[...] = _dot(xa, a1_ref[...]).astype(BF16)
    tg_ref[...] = jax.nn.sigmoid(_dot(xg, g1_ref[...])).astype(BF16)


def _prev_specs(tm, width, seq):
    tail = pl.BlockSpec((8, width), lambda i: (jnp.maximum(i * (tm // 8) - 1, 0), 0))
    first = pl.BlockSpec((1, 1, width), lambda i: (i * tm // seq, 0, 0))
    return tail, first


def _lora1(h, hprev, mu_wag, w1, a1, g1, seq=None):
    m, d = h.shape
    tm = _tile(m if seq is None else seq, 256, 16)
    nw, na, ng = w1.shape[1], a1.shape[1], g1.shape[1]
    row = lambda i: (i, 0)
    full = lambda i: (0, 0)
    if seq is None:
        prev_specs, prev_args = [pl.BlockSpec((tm, d), row)], [hprev]
    else:
        prev_specs, prev_args = list(_prev_specs(tm, d, seq)), [h, hprev]
    return pl.pallas_call(
        functools.partial(_lora1_kernel, seq=seq),
        out_shape=[jax.ShapeDtypeStruct((m, nw), BF16), jax.ShapeDtypeStruct((m, na), BF16),
                   jax.ShapeDtypeStruct((m, ng), BF16)],
        grid=(m // tm,),
        in_specs=[pl.BlockSpec((tm, d), row)] + prev_specs + [pl.BlockSpec(mu_wag.shape, full),
                  pl.BlockSpec((d, nw), full), pl.BlockSpec((d, na), full), pl.BlockSpec((d, ng), full)],
        out_specs=[pl.BlockSpec((tm, nw), row), pl.BlockSpec((tm, na), row), pl.BlockSpec((tm, ng), row)],
        compiler_params=_cparams("arbitrary"),
        name="rwkv_lora1",
    )(h, *prev_args, mu_wag, w1, a1, g1)


def _head_sums(x, bd):
    hi = x.astype(BF16)
    r1 = x - hi.astype(F32)
    mid = r1.astype(BF16)
    lo = (r1 - mid.astype(F32)).astype(BF16)
    return _dot(hi, bd) + _dot(mid, bd) + _dot(lo, bd)


def _block_diag_ones():
    a = lax.broadcasted_iota(jnp.int32, (LANES, LANES), 0) // HD_RWKV
    b = lax.broadcasted_iota(jnp.int32, (LANES, LANES), 1) // HD_RWKV
    return jnp.where(a == b, 1.0, 0.0).astype(BF16)


def _rwkv_prep_kernel(p_ref, pp_ref, *rest, seq):
    if seq is None:
        first_ref = None
        (tw_ref, ta_ref, tg_ref, w2_ref, a2_ref, g2_ref, mu_ref, vec_ref,
         r_ref, dec_ref, km_ref, v_ref, kk_ref, be_ref, g_ref, bo_ref) = rest
    else:
        (first_ref, tw_ref, ta_ref, tg_ref, w2_ref, a2_ref, g2_ref, mu_ref, vec_ref,
         r_ref, dec_ref, km_ref, v_ref, kk_ref, be_ref, g_ref, bo_ref) = rest
    dr = r_ref.shape[1]
    bd = _block_diag_ones()
    mu = mu_ref[...]
    vec = vec_ref[...]
    w0, a0, k_k, k_a, r_k = vec[0:1, :], vec[1:2, :], vec[2:3, :], vec[3:4, :], vec[4:5, :]
    p_all = p_ref[...]
    pp_all = pp_ref[...] if seq is None else _shift_rows(p_all, pp_ref, first_ref, seq)

    def shifted(c):
        pc = p_all[:, c * dr:(c + 1) * dr]
        return pc + (pp_all[:, c * dr:(c + 1) * dr] - pc) * mu[c:c + 1, :]

    r = shifted(0)
    k = shifted(1)
    v = shifted(2)
    w = -jnp.logaddexp(-(w0 + _dot(tw_ref[...], w2_ref[...])), 0.0) - 0.5
    dec = -jnp.exp(w)
    a = jax.nn.sigmoid(a0 + _dot(ta_ref[...], a2_ref[...]))
    g = _dot(tg_ref[...], g2_ref[...])
    kk = k * k_k
    kmod = k * (1.0 + (a - 1.0) * k_a)
    rk = r * kmod * r_k
    for c in range(dr // LANES):
        sl = slice(c * LANES, (c + 1) * LANES)
        kkc = kk[:, sl]
        kkc = kkc * lax.rsqrt(jnp.maximum(_head_sums(kkc * kkc, bd), 1e-24))
        kk_ref[:, sl] = kkc
        be_ref[:, sl] = kkc * a[:, sl]
        bo_ref[:, sl] = _head_sums(rk[:, sl], bd) * v[:, sl]
    r_ref[...] = r
    dec_ref[...] = dec
    km_ref[...] = kmod
    v_ref[...] = v
    g_ref[...] = g


def _rwkv_prep(p, pprev, tw, ta, tg, w2, a2, g2, mu_rkv, vecs, seq=None):
    m = p.shape[0]
    dr = p.shape[1] // 3
    tm = _tile(m if seq is None else seq, 128, 16)
    row = lambda i: (i, 0)
    full = lambda i: (0, 0)
    outs = [jax.ShapeDtypeStruct((m, dr), F32)] * 8
    if seq is None:
        prev_specs, prev_args = [pl.BlockSpec((tm, 3 * dr), row)], [pprev]
    else:
        prev_specs, prev_args = list(_prev_specs(tm, 3 * dr, seq)), [p, pprev]
    return pl.pallas_call(
        functools.partial(_rwkv_prep_kernel, seq=seq),
        out_shape=outs,
        grid=(m // tm,),
        in_specs=[pl.BlockSpec((tm, 3 * dr), row)] + prev_specs + [
                  pl.BlockSpec((tm, tw.shape[1]), row), pl.BlockSpec((tm, ta.shape[1]), row),
                  pl.BlockSpec((tm, tg.shape[1]), row),
                  pl.BlockSpec(w2.shape, full), pl.BlockSpec(a2.shape, full), pl.BlockSpec(g2.shape, full),
                  pl.BlockSpec(mu_rkv.shape, full), pl.BlockSpec(vecs.shape, full)],
        out_specs=[pl.BlockSpec((tm, dr), row)] * 8,
        compiler_params=_cparams("arbitrary"),
        name="rwkv_prep",
    )(p, *prev_args, tw, ta, tg, w2, a2, g2, mu_rkv, vecs)


SCAN_STEPS = 8


def _segment_ones(width):
    a = lax.broadcasted_iota(jnp.int32, (width, width), 0) // HD_RWKV
    b = lax.broadcasted_iota(jnp.int32, (width, width), 1) // HD_RWKV
    return jnp.where(a == b, 1.0, 0.0).astype(BF16)


def _rwkv_scan_kernel(kk_ref, dec_ref, be_ref, km_ref, r_ref, v_ref, s0_ref, y_ref, s_ref, *, tc, npair):
    c = pl.program_id(1)
    n = HD_RWKV
    gs = min(SCAN_STEPS, tc)
    wide = 2 if npair % 2 == 0 else 1
    assert tc % gs == 0

    @pl.when(c == 0)
    def _():
        s_ref[...] = s0_ref[...]

    seg = _segment_ones(wide * LANES)
    lane = lax.broadcasted_iota(jnp.int32, (n, LANES), 1)
    sub = lax.broadcasted_iota(jnp.int32, (n, LANES), 0)
    diag = (lane % n) == sub

    def stack(parts):
        rows = [jnp.concatenate(parts[i:i + wide], axis=1) for i in range(0, npair, wide)]
        return jnp.concatenate(rows, axis=0).astype(BF16)

    def part(full, p):
        return full[(p // wide) * n:(p // wide + 1) * n, (p % wide) * LANES:(p % wide + 1) * LANES]

    def steps(t0):
        def row(ref, p, r):
            return ref[pl.ds(t0, gs), p * LANES:(p + 1) * LANES][r:r + 1, :]

        y_rows = [[None] * gs for _ in range(npair)]

        def finish(y_all, r):
            for p in range(npair):
                y_rows[p][r] = jnp.sum(jnp.where(diag, part(y_all, p), 0.0), axis=0, keepdims=True)

        pending = None
        for r in range(gs):
            sa_all = _dot(stack([s_ref[0, p] * row(kk_ref, p, r) for p in range(npair)]), seg)
            vb_all = _dot(stack([jnp.where(diag, row(v_ref, p, r), 0.0) for p in range(npair)]), seg)
            if pending is not None:
                finish(pending, r - 1)
            ys = []
            for p in range(npair):
                s_new = (s_ref[0, p] * jnp.exp(row(dec_ref, p, r)) - part(sa_all, p) * row(be_ref, p, r)
                         + part(vb_all, p) * row(km_ref, p, r))
                s_ref[0, p] = s_new
                ys.append(s_new * row(r_ref, p, r))
            pending = _dot(stack(ys), seg)
        finish(pending, gs - 1)
        for p in range(npair):
            y_ref[pl.ds(t0, gs), p * LANES:(p + 1) * LANES] = jnp.concatenate(y_rows[p], axis=0)

    if tc == gs:
        steps(0)
    else:
        def body(i, carry):
            steps(pl.multiple_of(i * gs, gs))
            return carry

        lax.fori_loop(0, tc // gs, body, 0)


def _rwkv_scan(kk, dec, be, km, r, v, s0, batch, seq):
    m, dr = kk.shape
    npair = dr // LANES
    tc = _tile(seq, 128, 8)
    nc = seq // tc
    if seq == 1:
        kk, dec, be, km, r, v = (a.reshape(m, 1, dr) for a in (kk, dec, be, km, r, v))
        op = pl.BlockSpec((None, 1, dr), lambda b, c: (b, 0, 0))
        y_shape = jax.ShapeDtypeStruct((m, 1, dr), F32)
    else:
        op = pl.BlockSpec((tc, dr), lambda b, c: (b * nc + c, 0))
        y_shape = jax.ShapeDtypeStruct((m, dr), F32)
    kern = functools.partial(_rwkv_scan_kernel, tc=tc, npair=npair)
    st = pl.BlockSpec((1, npair, HD_RWKV, LANES), lambda b, c: (b, 0, 0, 0))
    y, s_fin = pl.pallas_call(
        kern,
        out_shape=[y_shape, jax.ShapeDtypeStruct((batch, npair, HD_RWKV, LANES), F32)],
        grid=(batch, nc),
        in_specs=[op] * 6 + [st],
        out_specs=[op, st],
        compiler_params=_cparams("arbitrary", "arbitrary"),
        name="rwkv_scan",
    )(kk, dec, be, km, r, v, s0)
    return y.reshape(m, dr), s_fin


CHUNK = 64


def _rwkv_chunk_kernel(kk_ref, ld_ref, be_ref, km_ref, r_ref, v_ref, s0_ref, y_ref, s_ref, *, npair):
    c = pl.program_id(1)
    L = CHUNK
    n = HD_RWKV

    @pl.when(c == 0)
    def _():
        s_ref[...] = s0_ref[...]

    lane = lax.broadcasted_iota(jnp.int32, (L, LANES), 1)
    trow = lax.broadcasted_iota(jnp.int32, (L, LANES), 0)
    head0 = lane < n
    src = lane % n
    strict = trow > src
    incl = trow >= src
    tri = (lax.broadcasted_iota(jnp.int32, (L, L), 0) >= lax.broadcasted_iota(jnp.int32, (L, L), 1))
    tri = jnp.where(tri, 1.0, 0.0).astype(BF16)
    bd_r = lax.broadcasted_iota(jnp.int32, (LANES, LANES), 0) // n
    bd_c = lax.broadcasted_iota(jnp.int32, (LANES, LANES), 1) // n
    bdmask = bd_r == bd_c

    def blk(x):
        return jnp.concatenate([jnp.where(head0, x, 0.0), jnp.where(head0, 0.0, x)], axis=0).astype(BF16)

    def bf(x):
        return x.astype(BF16)

    pairs = range(npair)
    sls = [slice(p * LANES, (p + 1) * LANES) for p in pairs]

    cum = []
    for p in pairs:
        ld = ld_ref[:, sls[p]]
        hi = bf(ld)
        r1 = ld - hi.astype(F32)
        mid = bf(r1)
        lo = bf(r1 - mid.astype(F32))
        cum.append(_dot(tri, hi) + _dot(tri, mid) + _dot(tri, lo))

    z, s0t = [], []
    for p in pairs:
        e_neg = jnp.exp(-cum[p])
        abar = -kk_ref[:, sls[p]] * jnp.exp(cum[p] - ld_ref[:, sls[p]])
        rbar = r_ref[:, sls[p]] * jnp.exp(cum[p])
        lhs = bf(jnp.concatenate([abar, rbar], axis=0))
        rhs = jnp.concatenate([blk(be_ref[:, sls[p]] * e_neg), blk(km_ref[:, sls[p]] * e_neg)], axis=0)
        z.append(_dot_nt(lhs, rhs))
        s0t.append(_dot_nt(lhs, bf(s_ref[0, p])))

    u, pw = [], []
    for p in pairs:
        amat = jnp.where(strict, z[p][:L, 2 * L:], 0.0)
        u.append(s0t[p][:L] + _dot(bf(amat), blk(v_ref[:, sls[p]])))
        pw.append(jnp.where(strict, z[p][:L, :2 * L], 0.0))

    n_fac = L.bit_length() - 1
    for it in range(n_fac):
        u = [u[p] + _dot(bf(pw[p]), blk(u[p])) for p in pairs]
        if it + 1 < n_fac:
            pw = [_dot(bf(pw[p]), blk(pw[p])) for p in pairs]

    for p in pairs:
        rbk = jnp.concatenate([jnp.where(incl, z[p][L:, :2 * L], 0.0), jnp.where(incl, z[p][L:, 2 * L:], 0.0)],
                              axis=1)
        uv = jnp.concatenate([blk(u[p]), blk(v_ref[:, sls[p]])], axis=0)
        y_ref[:, sls[p]] = s0t[p][L:] + _dot(bf(rbk), uv)

    for p in pairs:
        cl = cum[p][L - 1:L, :]
        e_end = jnp.exp(cl - cum[p])
        uv_t = jnp.transpose(jnp.concatenate([u[p], v_ref[:, sls[p]]], axis=0))
        ends = jnp.concatenate([be_ref[:, sls[p]] * e_end, km_ref[:, sls[p]] * e_end], axis=0)
        cmat = _dot(bf(uv_t), bf(ends))
        s_ref[0, p] = s_ref[0, p] * jnp.exp(cl) + jnp.where(bdmask, cmat, 0.0)


def _rwkv_chunked(kk, ld, be, km, r, v, s0, batch, seq):
    m, dr = kk.shape
    npair = dr // LANES
    assert seq % CHUNK == 0 and 2 * CHUNK == LANES
    nc = seq // CHUNK
    op = pl.BlockSpec((CHUNK, dr), lambda b, c: (b * nc + c, 0))
    st = pl.BlockSpec((1, npair, LANES, LANES), lambda b, c: (b, 0, 0, 0))
    return pl.pallas_call(
        functools.partial(_rwkv_chunk_kernel, npair=npair),
        out_shape=[jax.ShapeDtypeStruct((m, dr), F32), jax.ShapeDtypeStruct((batch, npair, LANES, LANES), F32)],
        grid=(batch, nc),
        in_specs=[op] * 6 + [st],
        out_specs=[op, st],
        compiler_params=_cparams("arbitrary", "arbitrary"),
        name="rwkv_chunked",
    )(kk, ld, be, km, r, v, s0)


def _rwkv_post_kernel(y_ref, bo_ref, g_ref, ln_ref, o_ref):
    bd = _block_diag_ones()
    ln = ln_ref[...]
    inv_n = 1.0 / HD_RWKV
    for c in range(y_ref.shape[1] // LANES):
        sl = slice(c * LANES, (c + 1) * LANES)
        y = y_ref[:, sl]
        mean = _head_sums(y, bd) * inv_n
        d = y - mean
        var = _head_sums(d * d, bd) * inv_n
        yn = d * lax.rsqrt(var + GN_EPS) * ln[0:1, sl] + ln[1:2, sl]
        o_ref[:, sl] = ((yn + bo_ref[:, sl]) * g_ref[:, sl]).astype(BF16)


def _rwkv_post(y, bonus, g, ln):
    m, dr = y.shape
    tm = _tile(m, 256, 16)
    row = lambda i: (i, 0)
    return pl.pallas_call(
        _rwkv_post_kernel,
        out_shape=jax.ShapeDtypeStruct((m, dr), BF16),
        grid=(m // tm,),
        in_specs=[pl.BlockSpec((tm, dr), row)] * 3 + [pl.BlockSpec(ln.shape, lambda i: (0, 0))],
        out_specs=pl.BlockSpec((tm, dr), row),
        compiler_params=_cparams("arbitrary"),
        name="rwkv_post",
    )(y, bonus, g, ln)


def _pack_state(s):
    b, h, n, _ = s.shape
    return s.reshape(b, h // 2, 2, n, n).transpose(0, 1, 3, 2, 4).reshape(b, h // 2, n, 2 * n)


def _unpack_state(s):
    b, p, n, _ = s.shape
    return s.reshape(b, p, n, 2, n).transpose(0, 1, 3, 2, 4).reshape(b, 2 * p, n, n)


def _blockdiag_state(s):
    b, h, n, _ = s.shape
    s = s.reshape(b, h // 2, 2, n, n)
    z = jnp.zeros_like(s[:, :, 0])
    top = jnp.concatenate([s[:, :, 0], z], axis=-1)
    bot = jnp.concatenate([z, s[:, :, 1]], axis=-1)
    return jnp.concatenate([top, bot], axis=-2)


def _blockdiag_to_heads(s):
    b, p, _, _ = s.shape
    n = HD_RWKV
    return jnp.stack([s[:, :, :n, :n], s[:, :, n:, n:]], axis=2).reshape(b, 2 * p, n, n)


def _pad_cols(w, n):
    return jnp.pad(w, ((0, 0), (0, n - w.shape[1])))


def _pad_rows(w, n):
    return jnp.pad(w, ((0, n - w.shape[0]), (0, 0)))


def _rope_tables(pos, rows_per_pos):
    half = HD_ATT // 2
    inv = ROPE_THETA ** (-jnp.arange(half, dtype=F32) / half)
    ang = pos.astype(F32)[:, None] * inv[None, :]
    cos = jnp.cos(ang)
    sin = jnp.sin(ang)
    cos2 = jnp.concatenate([cos, cos], axis=-1)
    sin2 = jnp.concatenate([-sin, sin], axis=-1)
    if rows_per_pos is not None:
        cos2 = jnp.broadcast_to(cos2, (rows_per_pos, HD_ATT))
        sin2 = jnp.broadcast_to(sin2, (rows_per_pos, HD_ATT))
    return cos2, sin2


def kernel(x_prompt, x_sample, cache_k, cache_v, state_shift, state_wkv, page_table, c_prompt, c_sample, g_norm, w_ada, b_ada, ffn_w_gate, ffn_w_up, ffn_w_down, w_in, w_out, mu_rkv, mu_wag, w0, w_lora1, w_lora2, a0, a_lora1, a_lora2, g_lora1, g_lora2, k_k, k_a, r_k, ln_x_w, ln_x_b, g_final):
    B, T, D = x_prompt.shape
    DB, DS, _ = x_sample.shape
    depth = g_norm.shape[0]
    assert DS == 1
    n_pages = page_table.shape[1]
    page = cache_k.shape[2]
    past_len = n_pages * page
    assert past_len % MOBA_BLOCK == 0 and MOBA_BLOCK == 2 * page
    da = cache_k.shape[3] * cache_k.shape[4]
    nh = da // HD_ATT
    dr = D - da
    n_slots = min(MOBA_TOPK, past_len // MOBA_BLOCK)

    xp = x_prompt.reshape(B * T, D)
    xs = x_sample.reshape(DB, D)
    pt_flat = page_table.reshape(-1).astype(jnp.int32)
    cos_p, sin_p = _rope_tables(jnp.arange(T, dtype=jnp.int32), None)
    cos_p = jnp.tile(cos_p, (B, 1))
    sin_p = jnp.tile(sin_p, (B, 1))

    outs = [[] for _ in range(8)]
    n_c = B + DB
    n_cp = -(-n_c // 8) * 8
    c_all = jnp.pad(jnp.concatenate([c_prompt, c_sample], axis=0), ((0, n_cp - n_c), (0, 0)))

    for l in range(depth):
        bf = lambda w: w.astype(BF16)
        wout = bf(w_out[l])
        lw = -(-w_lora1.shape[2] // LANES) * LANES
        la = -(-a_lora1.shape[2] // LANES) * LANES
        lg = -(-g_lora1.shape[2] // LANES) * LANES
        w1, a1, g1 = bf(_pad_cols(w_lora1[l], lw)), bf(_pad_cols(a_lora1[l], la)), bf(_pad_cols(g_lora1[l], lg))
        w2, a2, g2 = bf(_pad_rows(w_lora2[l], lw)), bf(_pad_rows(a_lora2[l], la)), bf(_pad_rows(g_lora2[l], lg))
        vecs = jnp.stack([w0[l], a0[l], k_k[l], k_a[l], r_k[l].reshape(dr)], axis=0)
        vecs = jnp.pad(vecs, ((0, 3), (0, 0)))
        ln = jnp.pad(jnp.stack([ln_x_w[l], ln_x_b[l]], axis=0), ((0, 6), (0, 0)))
        mu3 = jnp.pad(mu_rkv[l], ((0, 5), (0, 0)))
        muw = jnp.pad(mu_wag[l], ((0, 5), (0, 0)))

        mod = _ada(c_all, w_ada[l], b_ada[l])
        chunks = [mod[:, k * D:(k + 1) * D] for k in range(9)]
        mod_p = [ch[:B].reshape(B, 1, D) for ch in chunks]
        mod_s = [ch[B:B + DB].reshape(1, DB, D) for ch in chunks]

        def ffn(xp_, xs_, idx):
            sub = 2 * idx
            (hp_,) = _modulate(xp_, g_norm[l, sub], mod_p[3 * sub], mod_p[3 * sub + 1], T, False)
            (hs_,) = _modulate(xs_, g_norm[l, sub], mod_s[3 * sub], mod_s[3 * sub + 1], 1, False)
            u_p, wd_b, u_s = _ffn_up(hp_, ffn_w_gate, ffn_w_up, l, idx, ffn_w_down, rider=hs_)
            return _ffn_down(u_p, wd_b, xp_, mod_p[3 * sub + 2], T,
                             rider=(u_s, xs_, mod_s[3 * sub + 2].reshape(DB, D)))

        xp, xs = ffn(xp, xs, 0)

        hb, hf = _modulate(xp, g_norm[l, 1], mod_p[3], mod_p[4], T, True)
        hbs, hfs = _modulate(xs, g_norm[l, 1], mod_s[3], mod_s[4], 1, True)
        n_rid = 2 * DB + B
        n_rid_p = -(-n_rid // 16) * 16
        riders = jnp.concatenate([hbs, bf(state_shift[l]), jnp.zeros((n_rid_p - 2 * DB, D), BF16)], axis=0)
        cos_r, sin_r = _rope_tables(jnp.full((1,), past_len, dtype=jnp.int32), n_rid_p)
        q, q_r = _proj(hb, w_in, l, 0, da, (cos_p, sin_p), riders, (cos_r, sin_r))
        k, k_r = _proj(hb, w_in, l, da, da, (cos_p, sin_p), riders, (cos_r, sin_r))
        v, v_r = _proj(hb, w_in, l, 2 * da, da, None, riders)
        prk, p_r = _proj(hb, w_in, l, 3 * da, 3 * dr, None, riders)
        qs, ks, vs, prk_s = q_r[:DB], k_r[:DB], v_r[:DB], p_r[:DB]
        p_first = p_r[DB:2 * DB + B]
        att = _moba_prompt(q, k, v, B, T)

        h3 = hf.reshape(B, T, D)

        def rwkv(hf_, hprev_, prk_, pprev_, s0, nb_, nt_):
            seq = None if nt_ == 1 else nt_
            tw, ta, tg = _lora1(hf_, hprev_, muw, w1, a1, g1, seq)
            r_, dec_, km_, v_, kk_, be_, g_, bo_ = _rwkv_prep(prk_, pprev_, tw, ta, tg, w2, a2, g2, mu3, vecs, seq)
            if nt_ % CHUNK == 0:
                y, s_fin = _rwkv_chunked(kk_, dec_, be_, km_, r_, v_, _blockdiag_state(s0), nb_, nt_)
                s_fin = _blockdiag_to_heads(s_fin)
            else:
                y, s_fin = _rwkv_scan(kk_, dec_, be_, km_, r_, v_, _pack_state(s0), nb_, nt_)
                s_fin = _unpack_state(s_fin)
            rw = _rwkv_post(y, bo_, g_, ln)
            return rw, s_fin

        s0_p = jnp.zeros((B, dr // HD_RWKV, HD_RWKV, HD_RWKV), F32)
        rw_p, sfin_p = rwkv(hf, jnp.zeros((B, 1, D), F32), prk, p_first[DB:DB + B].reshape(B, 1, 3 * dr),
                            s0_p, B, T)
        xp = _out_proj(att, rw_p, wout, xp, mod_p[5], T)

        n_blocks = past_len // MOBA_BLOCK
        kmeans = _block_means(cache_k, l, pt_flat)
        sel = _select_blocks(kmeans, qs, DB, n_blocks, nh, n_slots)
        att_s = _moba_sample(qs, ks, vs, cache_k, cache_v, l, pt_flat, sel.reshape(-1), DB, n_pages, n_slots)

        rw_s, sfin_s = rwkv(hfs, state_shift[l], prk_s, p_first[:DB], state_wkv[l], DB, 1)
        xs = _out_proj(att_s, rw_s, wout, xs, mod_s[5], 1)
        xp, xs = ffn(xp, xs, 1)

        outs[0].append(k.reshape(B, T, nh, HD_ATT))
        outs[1].append(v.reshape(B, T, nh, HD_ATT))
        outs[2].append(h3[:, -1])
        outs[3].append(sfin_p)
        outs[4].append(ks.reshape(DB, 1, nh, HD_ATT))
        outs[5].append(vs.reshape(DB, 1, nh, HD_ATT))
        outs[6].append(hfs)
        outs[7].append(sfin_s)

    y_prompt = _rmsnorm(xp, g_final).reshape(B, T, D)
    y_sample = _rmsnorm(xs, g_final).reshape(DB, 1, D)
    return (y_prompt, y_sample) + tuple(jnp.stack(o) for o in outs)
```

```python
import functools

import jax
import jax.numpy as jnp
from jax import lax
from jax.experimental import pallas as pl
from jax.experimental.pallas import tpu as pltpu

F32 = jnp.float32
BF16 = jnp.bfloat16

HD_ATT = 128
HD_RWKV = 64
LANES = 128
MOBA_BLOCK = 256
MOBA_TOPK = 3
ROPE_THETA = 10000.0
RMS_EPS = 1e-6
GN_EPS = 64e-5
NEG_BIG = -1e30
VMEM_LIMIT = 56 * 1024 * 1024


def _cparams(*sem):
    return pltpu.CompilerParams(dimension_semantics=sem, vmem_limit_bytes=VMEM_LIMIT)


def _tile(n, pref, align):
    if n <= pref:
        return n
    t = (pref // align) * align
    while t >= align:
        if n % t == 0:
            return t
        t -= align
    return n


def _dot(a, b):
    return jnp.dot(a, b, preferred_element_type=F32)


def _dot_nt(a, b):
    return lax.dot_general(a, b, (((1,), (1,)), ((), ())), preferred_element_type=F32)


def _ada_kernel(c_ref, w_ref, b_ref, o_ref):
    c = c_ref[...]
    s = (c * jax.nn.sigmoid(c)).astype(BF16)
    o_ref[...] = _dot(s, w_ref[...].astype(BF16)) + b_ref[...]


def _ada(c, w, b):
    m, d = c.shape
    n = w.shape[1]
    tn = _tile(n, 512, LANES)
    return pl.pallas_call(
        _ada_kernel,
        out_shape=jax.ShapeDtypeStruct((m, n), F32),
        grid=(n // tn,),
        in_specs=[pl.BlockSpec((m, d), lambda j: (0, 0)),
                  pl.BlockSpec((d, tn), lambda j: (0, j)),
                  pl.BlockSpec((1, tn), lambda j: (0, j))],
        out_specs=pl.BlockSpec((m, tn), lambda j: (0, j)),
        compiler_params=_cparams("arbitrary"),
        name="ada_proj",
    )(c, w, b.reshape(1, n))


def _modulate_kernel(x_ref, g_ref, sh_ref, sc_ref, *o_refs):
    x = x_ref[...]
    y = x * lax.rsqrt(jnp.mean(x * x, axis=-1, keepdims=True) + RMS_EPS) * g_ref[...]
    h = y * (1.0 + sc_ref[0]) + sh_ref[0]
    o_refs[0][...] = h.astype(BF16)
    if len(o_refs) > 1:
        o_refs[1][...] = h


def _modulate(x, g, sh, sc, rows_per_group, want_f32):
    m, d = x.shape
    r = sh.shape[1]
    tm = m if r > 1 else _tile(rows_per_group, 512, 16)
    grp = (lambda i: (0, 0, 0)) if r > 1 else (lambda i: (i * tm // rows_per_group, 0, 0))
    out_shape = [jax.ShapeDtypeStruct((m, d), BF16)]
    out_specs = [pl.BlockSpec((tm, d), lambda i: (i, 0))]
    if want_f32:
        out_shape.append(jax.ShapeDtypeStruct((m, d), F32))
        out_specs.append(pl.BlockSpec((tm, d), lambda i: (i, 0)))
    return pl.pallas_call(
        _modulate_kernel,
        out_shape=out_shape,
        grid=(m // tm,),
        in_specs=[pl.BlockSpec((tm, d), lambda i: (i, 0)),
                  pl.BlockSpec((1, d), lambda i: (0, 0)),
                  pl.BlockSpec((1, r, d), grp),
                  pl.BlockSpec((1, r, d), grp)],
        out_specs=out_specs,
        compiler_params=_cparams("arbitrary"),
        name="modulate",
    )(x, g.reshape(1, d), sh, sc)


def _rmsnorm_kernel(x_ref, g_ref, o_ref):
    x = x_ref[...]
    o_ref[...] = x * lax.rsqrt(jnp.mean(x * x, axis=-1, keepdims=True) + RMS_EPS) * g_ref[...]


def _rmsnorm(x, g):
    m, d = x.shape
    tm = _tile(m, 512, 8)
    return pl.pallas_call(
        _rmsnorm_kernel,
        out_shape=jax.ShapeDtypeStruct((m, d), F32),
        grid=(m // tm,),
        in_specs=[pl.BlockSpec((tm, d), lambda i: (i, 0)), pl.BlockSpec((1, d), lambda i: (0, 0))],
        out_specs=pl.BlockSpec((tm, d), lambda i: (i, 0)),
        compiler_params=_cparams("arbitrary"),
        name="final_rmsnorm",
    )(x, g.reshape(1, d))


def _first_pass(nj):
    return lambda i, j: jnp.where(i == 0, j, nj - 1)


def _ffn_up_kernel(h_ref, wg_ref, wu_ref, *rest, has_wd, has_rider):
    rest = list(rest)
    wd_ref = rest.pop(0) if has_wd else None
    hs_ref = rest.pop(0) if has_rider else None
    o_ref = rest.pop(0)
    wdb_ref = rest.pop(0) if has_wd else None
    us_ref = rest.pop(0) if has_rider else None

    def gated(h):
        g = _dot(h, wg_ref[...].astype(BF16))
        return (g * jax.nn.sigmoid(g) * _dot(h, wu_ref[...].astype(BF16))).astype(BF16)

    if has_wd or has_rider:
        @pl.when(pl.program_id(0) == 0)
        def _():
            if has_wd:
                wdb_ref[...] = wd_ref[...].astype(BF16)
            if has_rider:
                us_ref[...] = gated(hs_ref[...])

    o_ref[...] = gated(h_ref[...])


def _ffn_up(h, wg_all, wu_all, layer, idx, wd_all=None, rider=None):
    m, d = h.shape
    f = wg_all.shape[-1]
    tm = _tile(m, 1024, 16)
    tn = _tile(f, 256, LANES)
    nj = f // tn
    fp = _first_pass(nj)
    w_spec = pl.BlockSpec((None, None, d, tn), lambda i, j: (layer, idx, 0, j))
    in_specs = [pl.BlockSpec((tm, d), lambda i, j: (i, 0)), w_spec, w_spec]
    out_shape = [jax.ShapeDtypeStruct((m, f), BF16)]
    out_specs = [pl.BlockSpec((tm, tn), lambda i, j: (i, j))]
    args = [h, wg_all, wu_all]
    if wd_all is not None:
        in_specs.append(pl.BlockSpec((None, None, tn, d), lambda i, j: (layer, idx, fp(i, j), 0)))
        out_shape.append(jax.ShapeDtypeStruct((f, d), BF16))
        out_specs.append(pl.BlockSpec((tn, d), lambda i, j: (fp(i, j), 0)))
        args.append(wd_all)
    if rider is not None:
        ms = rider.shape[0]
        in_specs.append(pl.BlockSpec((ms, d), lambda i, j: (0, 0)))
        out_shape.append(jax.ShapeDtypeStruct((ms, f), BF16))
        out_specs.append(pl.BlockSpec((ms, tn), lambda i, j: (0, fp(i, j))))
        args.append(rider)
    res = pl.pallas_call(
        functools.partial(_ffn_up_kernel, has_wd=wd_all is not None, has_rider=rider is not None),
        out_shape=out_shape,
        grid=(m // tm, nj),
        in_specs=in_specs,
        out_specs=out_specs,
        compiler_params=_cparams("arbitrary", "arbitrary"),
        name="ffn_up",
    )(*args)
    return res if len(res) > 1 else res[0]


def _ffn_down_kernel(u_ref, w_ref, x_ref, ga_ref, *rest):
    if len(rest) == 1:
        (o_ref,) = rest
    else:
        us_ref, xs_ref, gas_ref, o_ref, os_ref = rest

        @pl.when(pl.program_id(0) == 0)
        def _():
            os_ref[...] = xs_ref[...] + 0.5 * gas_ref[...] * _dot(us_ref[...], w_ref[...])

    o_ref[...] = x_ref[...] + 0.5 * ga_ref[0] * _dot(u_ref[...], w_ref[...])


def _ffn_down(u, wd, x, ga, rows_per_group, rider=None):
    m, f = u.shape
    d = wd.shape[1]
    r = ga.shape[1]
    tm = m if r > 1 else _tile(rows_per_group, 512, 16)
    tn = _tile(d, 256, LANES)
    nj = d // tn
    grp = (lambda i, j: (0, 0, j)) if r > 1 else (lambda i, j: (i * tm // rows_per_group, 0, j))
    in_specs = [pl.BlockSpec((tm, f), lambda i, j: (i, 0)),
                pl.BlockSpec((f, tn), lambda i, j: (0, j)),
                pl.BlockSpec((tm, tn), lambda i, j: (i, j)),
                pl.BlockSpec((1, r, tn), grp)]
    out_shape = [jax.ShapeDtypeStruct((m, d), F32)]
    out_specs = [pl.BlockSpec((tm, tn), lambda i, j: (i, j))]
    args = [u, wd, x, ga]
    if rider is not None:
        ms = rider[0].shape[0]
        fp = _first_pass(nj)
        tile_s = pl.BlockSpec((ms, tn), lambda i, j: (0, fp(i, j)))
        in_specs += [pl.BlockSpec((ms, f), lambda i, j: (0, 0)), tile_s, tile_s]
        out_shape.append(jax.ShapeDtypeStruct((ms, d), F32))
        out_specs.append(tile_s)
        args += list(rider)
    res = pl.pallas_call(
        _ffn_down_kernel,
        out_shape=out_shape,
        grid=(m // tm, nj),
        in_specs=in_specs,
        out_specs=out_specs,
        compiler_params=_cparams("arbitrary", "arbitrary"),
        name="ffn_down",
    )(*args)
    return res if rider is not None else res[0]


def _rope_store(o_ref, x, cos, sin):
    for c in range(x.shape[1] // HD_ATT):
        xc = x[:, c * HD_ATT:(c + 1) * HD_ATT]
        o_ref[:, c * HD_ATT:(c + 1) * HD_ATT] = xc * cos + pltpu.roll(xc, HD_ATT // 2, 1) * sin


def _proj_kernel(h_ref, w_ref, *rest, has_rope, has_rider):
    rest = list(rest)
    cos_ref, sin_ref = (rest.pop(0), rest.pop(0)) if has_rope else (None, None)
    hs_ref = rest.pop(0) if has_rider else None
    cs_ref, ss_ref = (rest.pop(0), rest.pop(0)) if has_rope and has_rider else (None, None)
    o_ref = rest.pop(0)
    os_ref = rest.pop(0) if has_rider else None

    def emit(out_ref, rows, cos_r, sin_r):
        x = _dot(rows, w_ref[...].astype(BF16))
        if has_rope:
            _rope_store(out_ref, x, cos_r[...], sin_r[...])
        else:
            out_ref[...] = x

    if has_rider:
        @pl.when(pl.program_id(0) == 0)
        def _():
            emit(os_ref, hs_ref[...], cs_ref, ss_ref)

    emit(o_ref, h_ref[...], cos_ref, sin_ref)


def _proj(h, w, layer, col0, ncols, rope=None, rider=None, rider_rope=None):
    m, d = h.shape
    tm = _tile(m, 1024, 16)
    tn = _tile(ncols, 512, LANES)
    assert col0 % tn == 0
    j0 = col0 // tn
    nj = ncols // tn
    in_specs = [pl.BlockSpec((tm, d), lambda i, j: (i, 0)),
                pl.BlockSpec((None, d, tn), lambda i, j: (layer, 0, j + j0))]
    args = [h, w]
    out_shape = [jax.ShapeDtypeStruct((m, ncols), F32)]
    out_specs = [pl.BlockSpec((tm, tn), lambda i, j: (i, j))]
    if rope is not None:
        in_specs += [pl.BlockSpec((tm, HD_ATT), lambda i, j: (i, 0))] * 2
        args += list(rope)
    if rider is not None:
        ms = rider.shape[0]
        fp = _first_pass(nj)
        in_specs.append(pl.BlockSpec((ms, d), lambda i, j: (0, 0)))
        args.append(rider)
        if rope is not None:
            in_specs += [pl.BlockSpec((ms, HD_ATT), lambda i, j: (0, 0))] * 2
            args += list(rider_rope)
        out_shape.append(jax.ShapeDtypeStruct((ms, ncols), F32))
        out_specs.append(pl.BlockSpec((ms, tn), lambda i, j: (0, fp(i, j))))
    res = pl.pallas_call(
        functools.partial(_proj_kernel, has_rope=rope is not None, has_rider=rider is not None),
        out_shape=out_shape,
        grid=(m // tm, nj),
        in_specs=in_specs,
        out_specs=out_specs,
        compiler_params=_cparams("arbitrary", "arbitrary"),
        name="in_proj_rope" if rope is not None else "in_proj",
    )(*args)
    return res if rider is not None else res[0]


def _out_proj_kernel(att_ref, rw_ref, wa_ref, wr_ref, x_ref, ga_ref, o_ref):
    y = _dot(att_ref[...], wa_ref[...]) + _dot(rw_ref[...], wr_ref[...])
    o_ref[...] = x_ref[...] + ga_ref[0] * y


def _out_proj(att, rw, w, x, ga, rows_per_group):
    m, da = att.shape
    dr = rw.shape[1]
    d = w.shape[1]
    assert da == dr
    r = ga.shape[1]
    tm = m if r > 1 else _tile(rows_per_group, 1024, 16)
    tn = _tile(d, 512, LANES)
    grp = (lambda i, j: (0, 0, j)) if r > 1 else (lambda i, j: (i * tm // rows_per_group, 0, j))
    return pl.pallas_call(
        _out_proj_kernel,
        out_shape=jax.ShapeDtypeStruct((m, d), F32),
        grid=(m // tm, d // tn),
        in_specs=[pl.BlockSpec((tm, da), lambda i, j: (i, 0)),
                  pl.BlockSpec((tm, dr), lambda i, j: (i, 0)),
                  pl.BlockSpec((da, tn), lambda i, j: (0, j)),
                  pl.BlockSpec((dr, tn), lambda i, j: (1, j)),
                  pl.BlockSpec((tm, tn), lambda i, j: (i, j)),
                  pl.BlockSpec((1, r, tn), grp)],
        out_specs=pl.BlockSpec((tm, tn), lambda i, j: (i, j)),
        compiler_params=_cparams("arbitrary", "arbitrary"),
        name="out_proj",
    )(att, rw, w, w, x, ga)


def _rank_select(g_rows, n_valid, topk):
    nb = len(g_rows)
    sel = []
    for j in range(nb):
        rank = jnp.zeros_like(g_rows[j])
        for n in range(nb):
            if n == j:
                continue
            beats = (g_rows[n] > g_rows[j]) if n > j else (g_rows[n] >= g_rows[j])
            rank = rank + jnp.where(beats, jnp.where(n < n_valid, 1.0, 0.0), 0.0)
        sel.append(rank < float(topk))
    return sel


MOBA_HEADS_PER_STEP = 8


def _moba_prompt_kernel(q_ref, k_ref, v_ref, o_ref, kb_ref, vt_ref, km_ref, m_ref, l_ref, acc_ref,
                        *, nb, scale, hp):
    i = pl.program_id(2)
    blk = MOBA_BLOCK
    heads = [slice(hh * HD_ATT, (hh + 1) * HD_ATT) for hh in range(hp)]

    @pl.when(i == 0)
    def _():
        km_ref[...] = jnp.zeros_like(km_ref)
        for hh in range(hp):
            for j in range(nb):
                kj = k_ref[j * blk:(j + 1) * blk, heads[hh]]
                km_ref[hh, j:j + 1, :] = jnp.mean(kj, axis=0, keepdims=True)
                kb_ref[hh, j] = kj.astype(BF16)
                vt_ref[hh, j] = jnp.transpose(v_ref[j * blk:(j + 1) * blk, heads[hh]]).astype(BF16)

    kpos = lax.broadcasted_iota(jnp.int32, (blk, blk), 0)
    qpos = lax.broadcasted_iota(jnp.int32, (blk, blk), 1)
    hs = range(hp)
    qf = [q_ref[:, heads[hh]] for hh in hs]
    gt = [_dot_nt(km_ref[hh].astype(BF16), qf[hh].astype(BF16)) for hh in hs]
    sel = [_rank_select([gt[hh][j:j + 1, :] for j in range(nb)], i, MOBA_TOPK) for hh in hs]
    qb = [(qf[hh] * scale).astype(BF16) for hh in hs]

    s_own = [_dot_nt(kb_ref[hh, i], qb[hh]) for hh in hs]
    p_own = []
    for hh in hs:
        s = jnp.where(kpos <= qpos, s_own[hh], NEG_BIG)
        m0 = jnp.max(s, axis=0, keepdims=True)
        p = jnp.exp(s - m0)
        m_ref[hh] = m0
        l_ref[hh] = jnp.sum(p, axis=0, keepdims=True)
        p_own.append(p.astype(BF16))
    for hh in hs:
        acc_ref[hh] = _dot(vt_ref[hh, i], p_own[hh])

    for j in range(nb - 1):
        @pl.when(j < i)
        def _(j=j):
            s_all = [_dot_nt(kb_ref[hh, j], qb[hh]) for hh in hs]
            p_all, alphas = [], []
            for hh in hs:
                s = jnp.where(sel[hh][j], s_all[hh], NEG_BIG)
                m_old = m_ref[hh]
                m_new = jnp.maximum(m_old, jnp.max(s, axis=0, keepdims=True))
                alpha = jnp.exp(m_old - m_new)
                p = jnp.exp(s - m_new)
                m_ref[hh] = m_new
                l_ref[hh] = alpha * l_ref[hh] + jnp.sum(p, axis=0, keepdims=True)
                p_all.append(p.astype(BF16))
                alphas.append(alpha)
            for hh in hs:
                acc_ref[hh] = alphas[hh] * acc_ref[hh] + _dot(vt_ref[hh, j], p_all[hh])

    for hh in hs:
        o_ref[:, heads[hh]] = jnp.transpose(acc_ref[hh] / l_ref[hh]).astype(BF16)


def _moba_prompt(q, k, v, batch, seq):
    m, da = q.shape
    nh = da // HD_ATT
    assert seq % MOBA_BLOCK == 0
    nb = seq // MOBA_BLOCK
    nbp = -(-nb // 8) * 8
    hp = _tile(nh, MOBA_HEADS_PER_STEP, 1)
    w = hp * HD_ATT
    kern = functools.partial(_moba_prompt_kernel, nb=nb, scale=HD_ATT ** -0.5, hp=hp)
    return pl.pallas_call(
        kern,
        out_shape=jax.ShapeDtypeStruct((m, da), BF16),
        grid=(batch, nh // hp, nb),
        in_specs=[pl.BlockSpec((MOBA_BLOCK, w), lambda b, h, i: (b * nb + i, h)),
                  pl.BlockSpec((seq, w), lambda b, h, i: (b, h)),
                  pl.BlockSpec((seq, w), lambda b, h, i: (b, h))],
        out_specs=pl.BlockSpec((MOBA_BLOCK, w), lambda b, h, i: (b * nb + i, h)),
        scratch_shapes=[pltpu.VMEM((hp, nb, MOBA_BLOCK, HD_ATT), BF16),
                        pltpu.VMEM((hp, nb, HD_ATT, MOBA_BLOCK), BF16),
                        pltpu.VMEM((hp, nbp, HD_ATT), F32),
                        pltpu.VMEM((hp, 1, MOBA_BLOCK), F32),
                        pltpu.VMEM((hp, 1, MOBA_BLOCK), F32),
                        pltpu.VMEM((hp, HD_ATT, MOBA_BLOCK), F32)],
        compiler_params=_cparams("arbitrary", "arbitrary", "arbitrary"),
        name="moba_prompt",
    )(q, k, v)


MEAN_BLOCKS_PER_STEP = 8


def _block_mean_kernel(pt_ref, *refs):
    o_ref = refs[-1]
    for n in range(len(refs) // 2):
        tot = jnp.sum(refs[2 * n][...], axis=0) + jnp.sum(refs[2 * n + 1][...], axis=0)
        o_ref[n] = tot * (1.0 / MOBA_BLOCK)


def _block_means(cache_k, layer, page_table_flat):
    _, _, page, nh, hd = cache_k.shape
    n = page_table_flat.shape[0] // 2
    per = _tile(n, MEAN_BLOCKS_PER_STEP, 1)

    def page_map(e):
        return lambda s, pt: (layer, pt[2 * per * s + e], 0, 0, 0)

    return pl.pallas_call(
        _block_mean_kernel,
        out_shape=jax.ShapeDtypeStruct((n, nh, hd), F32),
        grid_spec=pltpu.PrefetchScalarGridSpec(
            num_scalar_prefetch=1,
            grid=(n // per,),
            in_specs=[pl.BlockSpec((None, None, page, nh, hd), page_map(e)) for e in range(2 * per)],
            out_specs=pl.BlockSpec((per, nh, hd), lambda s, pt: (s, 0, 0))),
        compiler_params=_cparams("arbitrary"),
        name="block_key_means",
    )(page_table_flat, *([cache_k] * (2 * per)))


def _select_kernel(km_ref, q_ref, o_ref, *, n_blocks, topk):
    q = q_ref[0]
    g = [jnp.sum(km_ref[0, n] * q, axis=1, keepdims=True) for n in range(n_blocks)]
    for s in range(topk):
        mx = g[0]
        for n in range(1, n_blocks):
            mx = jnp.maximum(mx, g[n])
        pick = jnp.full(mx.shape, n_blocks, jnp.int32)
        for n in range(n_blocks - 1, -1, -1):
            pick = jnp.where(g[n] == mx, n, pick)
        o_ref[0, s] = pick
        g = [jnp.where(pick == n, -jnp.inf, g[n]) for n in range(n_blocks)]


def _select_blocks(kmeans, q, n_seq, n_blocks, nh, topk):
    hd = HD_ATT
    kern = functools.partial(_select_kernel, n_blocks=n_blocks, topk=topk)
    return pl.pallas_call(
        kern,
        out_shape=jax.ShapeDtypeStruct((n_seq, topk, nh, 1), jnp.int32),
        grid=(n_seq,),
        in_specs=[pl.BlockSpec((1, n_blocks, nh, hd), lambda b: (b, 0, 0, 0)),
                  pl.BlockSpec((1, nh, hd), lambda b: (b, 0, 0))],
        out_specs=pl.BlockSpec((1, topk, nh, 1), lambda b: (b, 0, 0, 0)),
        compiler_params=_cparams("arbitrary"),
        name="moba_select",
    )(kmeans.reshape(n_seq, n_blocks, nh, hd), q.reshape(n_seq, nh, hd))


def _moba_sample_kernel(pt_ref, idx_ref, q_ref, kn_ref, vn_ref, ck_hbm, cv_hbm, o_ref, kbuf, vbuf, sem,
                        *, layer, scale, n_slots, n_pages, nh, page, n_seq):
    b = pl.program_id(0)
    n_pg = 2 * n_slots

    def copies(seq, slot):
        out = []
        for h in range(nh):
            for s in range(n_slots):
                blk = idx_ref[(seq * n_slots + s) * nh + h]
                for half in range(2):
                    pid = pt_ref[seq * n_pages + 2 * blk + half]
                    j = s * 2 + half
                    out.append(pltpu.make_async_copy(ck_hbm.at[layer, pid, :, h, :], kbuf.at[slot, h, j],
                                                     sem.at[0, slot]))
                    out.append(pltpu.make_async_copy(cv_hbm.at[layer, pid, :, h, :], vbuf.at[slot, h, j],
                                                     sem.at[1, slot]))
        return out

    @pl.when(b == 0)
    def _():
        for cp in copies(0, 0):
            cp.start()

    slot = b % 2

    @pl.when(b + 1 < n_seq)
    def _():
        for cp in copies(b + 1, 1 - slot):
            cp.start()

    for cp in copies(b, slot):
        cp.wait()

    for h in range(nh):
        q = q_ref[0, :, h * HD_ATT:(h + 1) * HD_ATT]
        kn = kn_ref[0, :, h * HD_ATT:(h + 1) * HD_ATT]
        vn = vn_ref[0, :, h * HD_ATT:(h + 1) * HD_ATT]
        kh = kbuf[slot, h].reshape(n_pg * page, HD_ATT)
        vh = vbuf[slot, h].reshape(n_pg * page, HD_ATT)
        s0 = jnp.sum(q * kn, axis=1, keepdims=True) * scale
        s = jnp.sum(kh * q, axis=1, keepdims=True) * scale
        m = jnp.maximum(jnp.max(s, axis=0, keepdims=True), s0)
        p = jnp.exp(s - m)
        p0 = jnp.exp(s0 - m)
        den = jnp.sum(p, axis=0, keepdims=True) + p0
        num = jnp.sum(p * vh, axis=0, keepdims=True) + p0 * vn
        o_ref[0, :, h * HD_ATT:(h + 1) * HD_ATT] = (num / den).astype(BF16)


def _moba_sample(q, k_new, v_new, cache_k, cache_v, layer, page_table_flat, sel_idx_flat, n_seq, n_pages, n_slots):
    _, _, page, nh, hd = cache_k.shape
    da = q.shape[1]
    assert MOBA_BLOCK == 2 * page and hd == HD_ATT
    kern = functools.partial(_moba_sample_kernel, layer=layer, scale=HD_ATT ** -0.5, n_slots=n_slots,
                             n_pages=n_pages, nh=nh, page=page, n_seq=n_seq)
    row = pl.BlockSpec((1, 1, da), lambda b, pt, idx: (b, 0, 0))
    out = pl.pallas_call(
        kern,
        out_shape=jax.ShapeDtypeStruct((n_seq, 1, da), BF16),
        grid_spec=pltpu.PrefetchScalarGridSpec(
            num_scalar_prefetch=2,
            grid=(n_seq,),
            in_specs=[row, row, row, pl.BlockSpec(memory_space=pl.ANY), pl.BlockSpec(memory_space=pl.ANY)],
            out_specs=row,
            scratch_shapes=[pltpu.VMEM((2, nh, 2 * n_slots, page, hd), F32),
                            pltpu.VMEM((2, nh, 2 * n_slots, page, hd), F32),
                            pltpu.SemaphoreType.DMA((2, 2))]),
        compiler_params=_cparams("arbitrary"),
        name="moba_sample",
    )(page_table_flat, sel_idx_flat, q.reshape(n_seq, 1, da), k_new.reshape(n_seq, 1, da),
      v_new.reshape(n_seq, 1, da), cache_k, cache_v)
    return out.reshape(n_seq, da)


def _shift_rows(x, tail_ref, first_ref, seq):
    tm = x.shape[0]
    local = lax.broadcasted_iota(jnp.int32, (tm, 1), 0)
    seq_off = lax.rem(pl.program_id(0) * tm, seq)
    prev = jnp.where(local == 0, tail_ref[7:8, :], pltpu.roll(x, 1, 0))
    return jnp.where((local + seq_off) == 0, first_ref[0], prev)


def _lora1_kernel(h_ref, hp_ref, *rest, seq):
    if seq is None:
        mu_ref, w1_ref, a1_ref, g1_ref, tw_ref, ta_ref, tg_ref = rest
        h = h_ref[...]
        hp = hp_ref[...]
    else:
        first_ref, mu_ref, w1_ref, a1_ref, g1_ref, tw_ref, ta_ref, tg_ref = rest
        h = h_ref[...]
        hp = _shift_rows(h, hp_ref, first_ref, seq)
    dx = hp - h
    mu = mu_ref[...]
    xw = (h + dx * mu[0:1, :]).astype(BF16)
    xa = (h + dx * mu[1:2, :]).astype(BF16)
    xg = (h + dx * mu[2:3, :]).astype(BF16)
    tw_ref[...] = jnp.tanh(_dot(xw, w1_ref[...])).astype(BF16)
    ta_ref[...] = _dot(xa, a1_ref[...]).astype(BF16)
    tg_ref[...] = jax.nn.sigmoid(_dot(xg, g1_ref[...])).astype(BF16)


def _prev_specs(tm, width, seq):
    tail = pl.BlockSpec((8, width), lambda i: (jnp.maximum(i * (tm // 8) - 1, 0), 0))
    first = pl.BlockSpec((1, 1, width), lambda i: (i * tm // seq, 0, 0))
    return tail, first


def _lora1(h, hprev, mu_wag, w1, a1, g1, seq=None):
    m, d = h.shape
    tm = _tile(m if seq is None else seq, 256, 16)
    nw, na, ng = w1.shape[1], a1.shape[1], g1.shape[1]
    row = lambda i: (i, 0)
    full = lambda i: (0, 0)
    if seq is None:
        prev_specs, prev_args = [pl.BlockSpec((tm, d), row)], [hprev]
    else:
        prev_specs, prev_args = list(_prev_specs(tm, d, seq)), [h, hprev]
    return pl.pallas_call(
        functools.partial(_lora1_kernel, seq=seq),
        out_shape=[jax.ShapeDtypeStruct((m, nw), BF16), jax.ShapeDtypeStruct((m, na), BF16),
                   jax.ShapeDtypeStruct((m, ng), BF16)],
        grid=(m // tm,),
        in_specs=[pl.BlockSpec((tm, d), row)] + prev_specs + [pl.BlockSpec(mu_wag.shape, full),
                  pl.BlockSpec((d, nw), full), pl.BlockSpec((d, na), full), pl.BlockSpec((d, ng), full)],
        out_specs=[pl.BlockSpec((tm, nw), row), pl.BlockSpec((tm, na), row), pl.BlockSpec((tm, ng), row)],
        compiler_params=_cparams("arbitrary"),
        name="rwkv_lora1",
    )(h, *prev_args, mu_wag, w1, a1, g1)


def _head_sums(x, bd):
    hi = x.astype(BF16)
    r1 = x - hi.astype(F32)
    mid = r1.astype(BF16)
    lo = (r1 - mid.astype(F32)).astype(BF16)
    return _dot(hi, bd) + _dot(mid, bd) + _dot(lo, bd)


def _block_diag_ones():
    a = lax.broadcasted_iota(jnp.int32, (LANES, LANES), 0) // HD_RWKV
    b = lax.broadcasted_iota(jnp.int32, (LANES, LANES), 1) // HD_RWKV
    return jnp.where(a == b, 1.0, 0.0).astype(BF16)


def _rwkv_prep_kernel(p_ref, pp_ref, *rest, seq):
    if seq is None:
        first_ref = None
        (tw_ref, ta_ref, tg_ref, w2_ref, a2_ref, g2_ref, mu_ref, vec_ref,
         r_ref, dec_ref, km_ref, v_ref, kk_ref, be_ref, g_ref, bo_ref) = rest
    else:
        (first_ref, tw_ref, ta_ref, tg_ref, w2_ref, a2_ref, g2_ref, mu_ref, vec_ref,
         r_ref, dec_ref, km_ref, v_ref, kk_ref, be_ref, g_ref, bo_ref) = rest
    dr = r_ref.shape[1]
    bd = _block_diag_ones()
    mu = mu_ref[...]
    vec = vec_ref[...]
    w0, a0, k_k, k_a, r_k = vec[0:1, :], vec[1:2, :], vec[2:3, :], vec[3:4, :], vec[4:5, :]
    p_all = p_ref[...]
    pp_all = pp_ref[...] if seq is None else _shift_rows(p_all, pp_ref, first_ref, seq)

    def shifted(c):
        pc = p_all[:, c * dr:(c + 1) * dr]
        return pc + (pp_all[:, c * dr:(c + 1) * dr] - pc) * mu[c:c + 1, :]

    r = shifted(0)
    k = shifted(1)
    v = shifted(2)
    w = -jnp.logaddexp(-(w0 + _dot(tw_ref[...], w2_ref[...])), 0.0) - 0.5
    dec = -jnp.exp(w)
    a = jax.nn.sigmoid(a0 + _dot(ta_ref[...], a2_ref[...]))
    g = _dot(tg_ref[...], g2_ref[...])
    kk = k * k_k
    kmod = k * (1.0 + (a - 1.0) * k_a)
    rk = r * kmod * r_k
    for c in range(dr // LANES):
        sl = slice(c * LANES, (c + 1) * LANES)
        kkc = kk[:, sl]
        kkc = kkc * lax.rsqrt(jnp.maximum(_head_sums(kkc * kkc, bd), 1e-24))
        kk_ref[:, sl] = kkc
        be_ref[:, sl] = kkc * a[:, sl]
        bo_ref[:, sl] = _head_sums(rk[:, sl], bd) * v[:, sl]
    r_ref[...] = r
    dec_ref[...] = dec
    km_ref[...] = kmod
    v_ref[...] = v
    g_ref[...] = g


def _rwkv_prep(p, pprev, tw, ta, tg, w2, a2, g2, mu_rkv, vecs, seq=None):
    m = p.shape[0]
    dr = p.shape[1] // 3
    tm = _tile(m if seq is None else seq, 128, 16)
    row = lambda i: (i, 0)
    full = lambda i: (0, 0)
    outs = [jax.ShapeDtypeStruct((m, dr), F32)] * 8
    if seq is None:
        prev_specs, prev_args = [pl.BlockSpec((tm, 3 * dr), row)], [pprev]
    else:
        prev_specs, prev_args = list(_prev_specs(tm, 3 * dr, seq)), [p, pprev]
    return pl.pallas_call(
        functools.partial(_rwkv_prep_kernel, seq=seq),
        out_shape=outs,
        grid=(m // tm,),
        in_specs=[pl.BlockSpec((tm, 3 * dr), row)] + prev_specs + [
                  pl.BlockSpec((tm, tw.shape[1]), row), pl.BlockSpec((tm, ta.shape[1]), row),
                  pl.BlockSpec((tm, tg.shape[1]), row),
                  pl.BlockSpec(w2.shape, full), pl.BlockSpec(a2.shape, full), pl.BlockSpec(g2.shape, full),
                  pl.BlockSpec(mu_rkv.shape, full), pl.BlockSpec(vecs.shape, full)],
        out_specs=[pl.BlockSpec((tm, dr), row)] * 8,
        compiler_params=_cparams("arbitrary"),
        name="rwkv_prep",
    )(p, *prev_args, tw, ta, tg, w2, a2, g2, mu_rkv, vecs)


SCAN_STEPS = 8


def _segment_ones(width):
    a = lax.broadcasted_iota(jnp.int32, (width, width), 0) // HD_RWKV
    b = lax.broadcasted_iota(jnp.int32, (width, width), 1) // HD_RWKV
    return jnp.where(a == b, 1.0, 0.0).astype(BF16)


def _rwkv_scan_kernel(kk_ref, dec_ref, be_ref, km_ref, r_ref, v_ref, s0_ref, y_ref, s_ref, *, tc, npair):
    c = pl.program_id(1)
    n = HD_RWKV
    gs = min(SCAN_STEPS, tc)
    wide = 2 if npair % 2 == 0 else 1
    assert tc % gs == 0

    @pl.when(c == 0)
    def _():
        s_ref[...] = s0_ref[...]

    seg = _segment_ones(wide * LANES)
    lane = lax.broadcasted_iota(jnp.int32, (n, LANES), 1)
    sub = lax.broadcasted_iota(jnp.int32, (n, LANES), 0)
    diag = (lane % n) == sub

    def stack(parts):
        rows = [jnp.concatenate(parts[i:i + wide], axis=1) for i in range(0, npair, wide)]
        return jnp.concatenate(rows, axis=0).astype(BF16)

    def part(full, p):
        return full[(p // wide) * n:(p // wide + 1) * n, (p % wide) * LANES:(p % wide + 1) * LANES]

    def steps(t0):
        def row(ref, p, r):
            return ref[pl.ds(t0, gs), p * LANES:(p + 1) * LANES][r:r + 1, :]

        y_rows = [[None] * gs for _ in range(npair)]

        def finish(y_all, r):
            for p in range(npair):
                y_rows[p][r] = jnp.sum(jnp.where(diag, part(y_all, p), 0.0), axis=0, keepdims=True)

        pending = None
        for r in range(gs):
            sa_all = _dot(stack([s_ref[0, p] * row(kk_ref, p, r) for p in range(npair)]), seg)
            vb_all = _dot(stack([jnp.where(diag, row(v_ref, p, r), 0.0) for p in range(npair)]), seg)
            if pending is not None:
                finish(pending, r - 1)
            ys = []
            for p in range(npair):
                s_new = (s_ref[0, p] * jnp.exp(row(dec_ref, p, r)) - part(sa_all, p) * row(be_ref, p, r)
                         + part(vb_all, p) * row(km_ref, p, r))
                s_ref[0, p] = s_new
                ys.append(s_new * row(r_ref, p, r))
            pending = _dot(stack(ys), seg)
        finish(pending, gs - 1)
        for p in range(npair):
            y_ref[pl.ds(t0, gs), p * LANES:(p + 1) * LANES] = jnp.concatenate(y_rows[p], axis=0)

    if tc == gs:
        steps(0)
    else:
        def body(i, carry):
            steps(pl.multiple_of(i * gs, gs))
            return carry

        lax.fori_loop(0, tc // gs, body, 0)


def _rwkv_scan(kk, dec, be, km, r, v, s0, batch, seq):
    m, dr = kk.shape
    npair = dr // LANES
    tc = _tile(seq, 128, 8)
    nc = seq // tc
    if seq == 1:
        kk, dec, be, km, r, v = (a.reshape(m, 1, dr) for a in (kk, dec, be, km, r, v))
        op = pl.BlockSpec((None, 1, dr), lambda b, c: (b, 0, 0))
        y_shape = jax.ShapeDtypeStruct((m, 1, dr), F32)
    else:
        op = pl.BlockSpec((tc, dr), lambda b, c: (b * nc + c, 0))
        y_shape = jax.ShapeDtypeStruct((m, dr), F32)
    kern = functools.partial(_rwkv_scan_kernel, tc=tc, npair=npair)
    st = pl.BlockSpec((1, npair, HD_RWKV, LANES), lambda b, c: (b, 0, 0, 0))
    y, s_fin = pl.pallas_call(
        kern,
        out_shape=[y_shape, jax.ShapeDtypeStruct((batch, npair, HD_RWKV, LANES), F32)],
        grid=(batch, nc),
        in_specs=[op] * 6 + [st],
        out_specs=[op, st],
        compiler_params=_cparams("arbitrary", "arbitrary"),
        name="rwkv_scan",
    )(kk, dec, be, km, r, v, s0)
    return y.reshape(m, dr), s_fin


CHUNK = 64


def _rwkv_chunk_kernel(kk_ref, ld_ref, be_ref, km_ref, r_ref, v_ref, s0_ref, y_ref, s_ref, *, npair):
    c = pl.program_id(1)
    L = CHUNK
    n = HD_RWKV

    @pl.when(c == 0)
    def _():
        s_ref[...] = s0_ref[...]

    lane = lax.broadcasted_iota(jnp.int32, (L, LANES), 1)
    trow = lax.broadcasted_iota(jnp.int32, (L, LANES), 0)
    head0 = lane < n
    src = lane % n
    strict = trow > src
    incl = trow >= src
    tri = (lax.broadcasted_iota(jnp.int32, (L, L), 0) >= lax.broadcasted_iota(jnp.int32, (L, L), 1))
    tri = jnp.where(tri, 1.0, 0.0).astype(BF16)
    bd_r = lax.broadcasted_iota(jnp.int32, (LANES, LANES), 0) // n
    bd_c = lax.broadcasted_iota(jnp.int32, (LANES, LANES), 1) // n
    bdmask = bd_r == bd_c

    def blk(x):
        return jnp.concatenate([jnp.where(head0, x, 0.0), jnp.where(head0, 0.0, x)], axis=0).astype(BF16)

    def bf(x):
        return x.astype(BF16)

    pairs = range(npair)
    sls = [slice(p * LANES, (p + 1) * LANES) for p in pairs]

    cum = []
    for p in pairs:
        ld = ld_ref[:, sls[p]]
        hi = bf(ld)
        r1 = ld - hi.astype(F32)
        mid = bf(r1)
        lo = bf(r1 - mid.astype(F32))
        cum.append(_dot(tri, hi) + _dot(tri, mid) + _dot(tri, lo))

    z, s0t = [], []
    for p in pairs:
        e_neg = jnp.exp(-cum[p])
        abar = -kk_ref[:, sls[p]] * jnp.exp(cum[p] - ld_ref[:, sls[p]])
        rbar = r_ref[:, sls[p]] * jnp.exp(cum[p])
        lhs = bf(jnp.concatenate([abar, rbar], axis=0))
        rhs = jnp.concatenate([blk(be_ref[:, sls[p]] * e_neg), blk(km_ref[:, sls[p]] * e_neg)], axis=0)
        z.append(_dot_nt(lhs, rhs))
        s0t.append(_dot_nt(lhs, bf(s_ref[0, p])))

    u, pw = [], []
    for p in pairs:
        amat = jnp.where(strict, z[p][:L, 2 * L:], 0.0)
        u.append(s0t[p][:L] + _dot(bf(amat), blk(v_ref[:, sls[p]])))
        pw.append(jnp.where(strict, z[p][:L, :2 * L], 0.0))

    n_fac = L.bit_length() - 1
    for it in range(n_fac):
        u = [u[p] + _dot(bf(pw[p]), blk(u[p])) for p in pairs]
        if it + 1 < n_fac:
            pw = [_dot(bf(pw[p]), blk(pw[p])) for p in pairs]

    for p in pairs:
        rbk = jnp.concatenate([jnp.where(incl, z[p][L:, :2 * L], 0.0), jnp.where(incl, z[p][L:, 2 * L:], 0.0)],
                              axis=1)
        uv = jnp.concatenate([blk(u[p]), blk(v_ref[:, sls[p]])], axis=0)
        y_ref[:, sls[p]] = s0t[p][L:] + _dot(bf(rbk), uv)

    for p in pairs:
        cl = cum[p][L - 1:L, :]
        e_end = jnp.exp(cl - cum[p])
        uv_t = jnp.transpose(jnp.concatenate([u[p], v_ref[:, sls[p]]], axis=0))
        ends = jnp.concatenate([be_ref[:, sls[p]] * e_end, km_ref[:, sls[p]] * e_end], axis=0)
        cmat = _dot(bf(uv_t), bf(ends))
        s_ref[0, p] = s_ref[0, p] * jnp.exp(cl) + jnp.where(bdmask, cmat, 0.0)


def _rwkv_chunked(kk, ld, be, km, r, v, s0, batch, seq):
    m, dr = kk.shape
    npair = dr // LANES
    assert seq % CHUNK == 0 and 2 * CHUNK == LANES
    nc = seq // CHUNK
    op = pl.BlockSpec((CHUNK, dr), lambda b, c: (b * nc + c, 0))
    st = pl.BlockSpec((1, npair, LANES, LANES), lambda b, c: (b, 0, 0, 0))
    return pl.pallas_call(
        functools.partial(_rwkv_chunk_kernel, npair=npair),
        out_shape=[jax.ShapeDtypeStruct((m, dr), F32), jax.ShapeDtypeStruct((batch, npair, LANES, LANES), F32)],
        grid=(batch, nc),
        in_specs=[op] * 6 + [st],
        out_specs=[op, st],
        compiler_params=_cparams("arbitrary", "arbitrary"),
        name="rwkv_chunked",
    )(kk, ld, be, km, r, v, s0)


def _rwkv_post_kernel(y_ref, bo_ref, g_ref, ln_ref, o_ref):
    bd = _block_diag_ones()
    ln = ln_ref[...]
    inv_n = 1.0 / HD_RWKV
    for c in range(y_ref.shape[1] // LANES):
        sl = slice(c * LANES, (c + 1) * LANES)
        y = y_ref[:, sl]
        mean = _head_sums(y, bd) * inv_n
        d = y - mean
        var = _head_sums(d * d, bd) * inv_n
        yn = d * lax.rsqrt(var + GN_EPS) * ln[0:1, sl] + ln[1:2, sl]
        o_ref[:, sl] = ((yn + bo_ref[:, sl]) * g_ref[:, sl]).astype(BF16)


def _rwkv_post(y, bonus, g, ln):
    m, dr = y.shape
    tm = _tile(m, 512, 16)
    row = lambda i: (i, 0)
    return pl.pallas_call(
        _rwkv_post_kernel,
        out_shape=jax.ShapeDtypeStruct((m, dr), BF16),
        grid=(m // tm,),
        in_specs=[pl.BlockSpec((tm, dr), row)] * 3 + [pl.BlockSpec(ln.shape, lambda i: (0, 0))],
        out_specs=pl.BlockSpec((tm, dr), row),
        compiler_params=_cparams("arbitrary"),
        name="rwkv_post",
    )(y, bonus, g, ln)


def _pack_state(s):
    b, h, n, _ = s.shape
    return s.reshape(b, h // 2, 2, n, n).transpose(0, 1, 3, 2, 4).reshape(b, h // 2, n, 2 * n)


def _unpack_state(s):
    b, p, n, _ = s.shape
    return s.reshape(b, p, n, 2, n).transpose(0, 1, 3, 2, 4).reshape(b, 2 * p, n, n)


def _blockdiag_state(s):
    b, h, n, _ = s.shape
    s = s.reshape(b, h // 2, 2, n, n)
    z = jnp.zeros_like(s[:, :, 0])
    top = jnp.concatenate([s[:, :, 0], z], axis=-1)
    bot = jnp.concatenate([z, s[:, :, 1]], axis=-1)
    return jnp.concatenate([top, bot], axis=-2)


def _blockdiag_to_heads(s):
    b, p, _, _ = s.shape
    n = HD_RWKV
    return jnp.stack([s[:, :, :n, :n], s[:, :, n:, n:]], axis=2).reshape(b, 2 * p, n, n)


def _pad_cols(w, n):
    return jnp.pad(w, ((0, 0), (0, n - w.shape[1])))


def _pad_rows(w, n):
    return jnp.pad(w, ((0, n - w.shape[0]), (0, 0)))


def _rope_tables(pos, rows_per_pos):
    half = HD_ATT // 2
    inv = ROPE_THETA ** (-jnp.arange(half, dtype=F32) / half)
    ang = pos.astype(F32)[:, None] * inv[None, :]
    cos = jnp.cos(ang)
    sin = jnp.sin(ang)
    cos2 = jnp.concatenate([cos, cos], axis=-1)
    sin2 = jnp.concatenate([-sin, sin], axis=-1)
    if rows_per_pos is not None:
        cos2 = jnp.broadcast_to(cos2, (rows_per_pos, HD_ATT))
        sin2 = jnp.broadcast_to(sin2, (rows_per_pos, HD_ATT))
    return cos2, sin2


def kernel(x_prompt, x_sample, cache_k, cache_v, state_shift, state_wkv, page_table, c_prompt, c_sample, g_norm, w_ada, b_ada, ffn_w_gate, ffn_w_up, ffn_w_down, w_in, w_out, mu_rkv, mu_wag, w0, w_lora1, w_lora2, a0, a_lora1, a_lora2, g_lora1, g_lora2, k_k, k_a, r_k, ln_x_w, ln_x_b, g_final):
    B, T, D = x_prompt.shape
    DB, DS, _ = x_sample.shape
    depth = g_norm.shape[0]
    assert DS == 1
    n_pages = page_table.shape[1]
    page = cache_k.shape[2]
    past_len = n_pages * page
    assert past_len % MOBA_BLOCK == 0 and MOBA_BLOCK == 2 * page
    da = cache_k.shape[3] * cache_k.shape[4]
    nh = da // HD_ATT
    dr = D - da
    n_slots = min(MOBA_TOPK, past_len // MOBA_BLOCK)

    xp = x_prompt.reshape(B * T, D)
    xs = x_sample.reshape(DB, D)
    pt_flat = page_table.reshape(-1).astype(jnp.int32)
    cos_p, sin_p = _rope_tables(jnp.arange(T, dtype=jnp.int32), None)
    cos_p = jnp.tile(cos_p, (B, 1))
    sin_p = jnp.tile(sin_p, (B, 1))

    outs = [[] for _ in range(8)]
    n_c = B + DB
    n_cp = -(-n_c // 8) * 8
    c_all = jnp.pad(jnp.concatenate([c_prompt, c_sample], axis=0), ((0, n_cp - n_c), (0, 0)))

    for l in range(depth):
        bf = lambda w: w.astype(BF16)
        wout = bf(w_out[l])
        lw = -(-w_lora1.shape[2] // LANES) * LANES
        la = -(-a_lora1.shape[2] // LANES) * LANES
        lg = -(-g_lora1.shape[2] // LANES) * LANES
        w1, a1, g1 = bf(_pad_cols(w_lora1[l], lw)), bf(_pad_cols(a_lora1[l], la)), bf(_pad_cols(g_lora1[l], lg))
        w2, a2, g2 = bf(_pad_rows(w_lora2[l], lw)), bf(_pad_rows(a_lora2[l], la)), bf(_pad_rows(g_lora2[l], lg))
        vecs = jnp.stack([w0[l], a0[l], k_k[l], k_a[l], r_k[l].reshape(dr)], axis=0)
        vecs = jnp.pad(vecs, ((0, 3), (0, 0)))
        ln = jnp.pad(jnp.stack([ln_x_w[l], ln_x_b[l]], axis=0), ((0, 6), (0, 0)))
        mu3 = jnp.pad(mu_rkv[l], ((0, 5), (0, 0)))
        muw = jnp.pad(mu_wag[l], ((0, 5), (0, 0)))

        mod = _ada(c_all, w_ada[l], b_ada[l])
        chunks = [mod[:, k * D:(k + 1) * D] for k in range(9)]
        mod_p = [ch[:B].reshape(B, 1, D) for ch in chunks]
        mod_s = [ch[B:B + DB].reshape(1, DB, D) for ch in chunks]

        def ffn(xp_, xs_, idx):
            sub = 2 * idx
            (hp_,) = _modulate(xp_, g_norm[l, sub], mod_p[3 * sub], mod_p[3 * sub + 1], T, False)
            (hs_,) = _modulate(xs_, g_norm[l, sub], mod_s[3 * sub], mod_s[3 * sub + 1], 1, False)
            u_p, wd_b, u_s = _ffn_up(hp_, ffn_w_gate, ffn_w_up, l, idx, ffn_w_down, rider=hs_)
            return _ffn_down(u_p, wd_b, xp_, mod_p[3 * sub + 2], T,
                             rider=(u_s, xs_, mod_s[3 * sub + 2].reshape(DB, D)))

        xp, xs = ffn(xp, xs, 0)

        hb, hf = _modulate(xp, g_norm[l, 1], mod_p[3], mod_p[4], T, True)
        hbs, hfs = _modulate(xs, g_norm[l, 1], mod_s[3], mod_s[4], 1, True)
        n_rid = 2 * DB + B
        n_rid_p = -(-n_rid // 16) * 16
        riders = jnp.concatenate([hbs, bf(state_shift[l]), jnp.zeros((n_rid_p - 2 * DB, D), BF16)], axis=0)
        cos_r, sin_r = _rope_tables(jnp.full((1,), past_len, dtype=jnp.int32), n_rid_p)
        q, q_r = _proj(hb, w_in, l, 0, da, (cos_p, sin_p), riders, (cos_r, sin_r))
        k, k_r = _proj(hb, w_in, l, da, da, (cos_p, sin_p), riders, (cos_r, sin_r))
        v, v_r = _proj(hb, w_in, l, 2 * da, da, None, riders)
        prk, p_r = _proj(hb, w_in, l, 3 * da, 3 * dr, None, riders)
        qs, ks, vs, prk_s = q_r[:DB], k_r[:DB], v_r[:DB], p_r[:DB]
        p_first = p_r[DB:2 * DB + B]
        att = _moba_prompt(q, k, v, B, T)

        h3 = hf.reshape(B, T, D)

        def rwkv(hf_, hprev_, prk_, pprev_, s0, nb_, nt_):
            seq = None if nt_ == 1 else nt_
            tw, ta, tg = _lora1(hf_, hprev_, muw, w1, a1, g1, seq)
            r_, dec_, km_, v_, kk_, be_, g_, bo_ = _rwkv_prep(prk_, pprev_, tw, ta, tg, w2, a2, g2, mu3, vecs, seq)
            if nt_ % CHUNK == 0:
                y, s_fin = _rwkv_chunked(kk_, dec_, be_, km_, r_, v_, _blockdiag_state(s0), nb_, nt_)
                s_fin = _blockdiag_to_heads(s_fin)
            else:
                y, s_fin = _rwkv_scan(kk_, dec_, be_, km_, r_, v_, _pack_state(s0), nb_, nt_)
                s_fin = _unpack_state(s_fin)
            rw = _rwkv_post(y, bo_, g_, ln)
            return rw, s_fin

        s0_p = jnp.zeros((B, dr // HD_RWKV, HD_RWKV, HD_RWKV), F32)
        rw_p, sfin_p = rwkv(hf, jnp.zeros((B, 1, D), F32), prk, p_first[DB:DB + B].reshape(B, 1, 3 * dr),
                            s0_p, B, T)
        xp = _out_proj(att, rw_p, wout, xp, mod_p[5], T)

        n_blocks = past_len // MOBA_BLOCK
        kmeans = _block_means(cache_k, l, pt_flat)
        sel = _select_blocks(kmeans, qs, DB, n_blocks, nh, n_slots)
        att_s = _moba_sample(qs, ks, vs, cache_k, cache_v, l, pt_flat, sel.reshape(-1), DB, n_pages, n_slots)

        rw_s, sfin_s = rwkv(hfs, state_shift[l], prk_s, p_first[:DB], state_wkv[l], DB, 1)
        xs = _out_proj(att_s, rw_s, wout, xs, mod_s[5], 1)
        xp, xs = ffn(xp, xs, 1)

        outs[0].append(k.reshape(B, T, nh, HD_ATT))
        outs[1].append(v.reshape(B, T, nh, HD_ATT))
        outs[2].append(h3[:, -1])
        outs[3].append(sfin_p)
        outs[4].append(ks.reshape(DB, 1, nh, HD_ATT))
        outs[5].append(vs.reshape(DB, 1, nh, HD_ATT))
        outs[6].append(hfs)
        outs[7].append(sfin_s)

    y_prompt = _rmsnorm(xp, g_final).reshape(B, T, D)
    y_sample = _rmsnorm(xs, g_final).reshape(DB, 1, D)
    return (y_prompt, y_sample) + tuple(jnp.stack(o) for o in outs)
```
